```python
import jax
import jax.numpy as jnp
from jax import lax
import numpy as np

D_MODEL = 2048
BATCH = 16
SEQ = 2048
DEPTH = 4

CHUNK = 64
N_MEM = 256
RMS_EPS = 1e-6
MAX_STREAM_CHUNKS = 1024

GMLP_BLOCK = 128
A_WIDTH = D_MODEL // 2
A_GROUPS = 8
A_GROUP_DIM = A_WIDTH // A_GROUPS

B_WIDTH = D_MODEL - A_WIDTH
B_GROUPS = 8
CONV_WIDTH = 3
AB_IN = 2 * A_WIDTH + 3 * B_WIDTH

C_HEADS = D_MODEL // 128
C_NOPE = 128
C_ROPE = 64
C_V = 128
C_Q_RANK = 512
C_KV_RANK = 256
C_IN = C_Q_RANK + C_KV_RANK + C_ROPE
ROPE_THETA = 10000.0
Q_BLOCK = 128

MEM_HEADS = 4
MEM_HEAD_DIM = D_MODEL // MEM_HEADS

D_FF = 4 * D_MODEL

N_EVEN = (DEPTH + 1) // 2
N_ODD = DEPTH // 2

kernel_name = 'hybrid_gmlp_shortconv_mla_memxattn_trunk'


def rmsnorm(x, g):
    xf = x.astype(jnp.float32)
    y = xf * lax.rsqrt(jnp.mean(xf * xf, axis=-1, keepdims=True) + RMS_EPS)
    return (y * g.astype(jnp.float32)).astype(x.dtype)


def gmlp_spatial_gate(u, v, v_norm_g, w_s, b_s):
    bsz, seq, _ = u.shape
    nb = seq // GMLP_BLOCK
    vg = v.reshape(bsz, seq, A_GROUPS, A_GROUP_DIM)
    vg = rmsnorm(vg, v_norm_g.reshape(A_GROUPS, A_GROUP_DIM))
    vg = vg.reshape(bsz, nb, GMLP_BLOCK, A_GROUPS, A_GROUP_DIM)
    cid = jnp.arange(GMLP_BLOCK) // CHUNK
    mask = (cid[None, :] <= cid[:, None]).astype(w_s.dtype)
    w = w_s * mask[None]
    mixed = jnp.einsum('gij,bnjgc->bnigc', w, vg) + b_s.T[None, None, :, :, None]
    return u * mixed.reshape(bsz, seq, A_WIDTH)


def gated_short_conv(bg, cg, h, conv_w):
    z = cg * h
    seq = z.shape[1]
    zp = jnp.pad(z, ((0, 0), (CONV_WIDTH - 1, 0), (0, 0)))
    conv = sum(conv_w[k] * zp[:, k:k + seq] for k in range(CONV_WIDTH))
    return bg * conv


def mixer_gmlp_conv(xn, w_in, v_norm_g, w_s, b_s, conv_w, w_out):
    z = xn @ w_in
    u, v, bg, cg, h = jnp.split(
        z, [A_WIDTH, 2 * A_WIDTH, 2 * A_WIDTH + B_WIDTH, 2 * A_WIDTH + 2 * B_WIDTH], axis=-1)
    y_a = gmlp_spatial_gate(jax.nn.gelu(u), jax.nn.gelu(v), v_norm_g, w_s, b_s)
    y_b = gated_short_conv(bg, cg, h, conv_w)
    return jnp.concatenate([y_a, y_b], axis=-1) @ w_out


def rope_tables(positions):
    inv = ROPE_THETA ** (-jnp.arange(0, C_ROPE, 2, dtype=jnp.float32) / C_ROPE)
    ang = positions.astype(jnp.float32)[..., None] * inv
    return jnp.cos(ang), jnp.sin(ang)


def apply_rope(x, cos, sin):
    half = x.shape[-1] // 2
    x1 = x[..., :half].astype(jnp.float32)
    x2 = x[..., half:].astype(jnp.float32)
    return jnp.concatenate([x1 * cos - x2 * sin, x2 * cos + x1 * sin], axis=-1).astype(x.dtype)


def mixer_mla(xn, positions, w_in, q_norm_g, kv_norm_g, w_uq, w_ukv, w_out):
    bsz, seq, _ = xn.shape
    z = xn @ w_in
    c_q, c_kv, k_rope = jnp.split(z, [C_Q_RANK, C_Q_RANK + C_KV_RANK], axis=-1)
    q = (rmsnorm(c_q, q_norm_g) @ w_uq).reshape(bsz, seq, C_HEADS, C_NOPE + C_ROPE)
    kv = (rmsnorm(c_kv, kv_norm_g) @ w_ukv).reshape(bsz, seq, C_HEADS, C_NOPE + C_V)
    cos, sin = rope_tables(positions)
    q_nope = q[..., :C_NOPE]
    q_rope = apply_rope(q[..., C_NOPE:], cos[:, :, None], sin[:, :, None])
    k_nope = kv[..., :C_NOPE]
    v = kv[..., C_NOPE:]
    k_rope = apply_rope(k_rope, cos, sin)
    scale = (C_NOPE + C_ROPE) ** -0.5
    chunk_id = jnp.arange(seq) // CHUNK
    outs = []
    for qb in range(seq // Q_BLOCK):
        q0, q1 = qb * Q_BLOCK, (qb + 1) * Q_BLOCK
        s = (jnp.einsum('bqhd,bkhd->bhqk', q_nope[:, q0:q1], k_nope[:, :q1])
             + jnp.einsum('bqhr,bkr->bhqk', q_rope[:, q0:q1], k_rope[:, :q1]))
        s = s.astype(jnp.float32) * scale
        mask = chunk_id[None, :q1] <= chunk_id[q0:q1, None]
        s = jnp.where(mask[None, None], s, -jnp.inf)
        p = jax.nn.softmax(s, axis=-1).astype(v.dtype)
        outs.append(jnp.einsum('bhqk,bkhd->bqhd', p, v[:, :q1]))
    o = jnp.concatenate(outs, axis=1).reshape(bsz, seq, C_HEADS * C_V)
    return o @ w_out


def mem_cross_attention(xn, memn, wq, wk, wv, wo):
    bsz, seq, _ = xn.shape
    n_mem = memn.shape[1]
    q = (xn @ wq).reshape(bsz, seq, MEM_HEADS, MEM_HEAD_DIM)
    k = (memn @ wk).reshape(bsz, n_mem, MEM_HEADS, MEM_HEAD_DIM)
    v = (memn @ wv).reshape(bsz, n_mem, MEM_HEADS, MEM_HEAD_DIM)
    s = jnp.einsum('bqhd,bmhd->bhqm', q, k).astype(jnp.float32) * (MEM_HEAD_DIM ** -0.5)
    p = jax.nn.softmax(s, axis=-1).astype(v.dtype)
    o = jnp.einsum('bhqm,bmhd->bqhd', p, v).reshape(bsz, seq, D_MODEL)
    return o @ wo


def squared_relu_mlp(xn, w1, w2):
    h = jax.nn.relu(xn @ w1)
    return (h * h) @ w2


def _normal(key, shape, scale):
    return jax.random.normal(key, shape, jnp.float32) * scale


def _gain(key, shape):
    return 1.0 + 0.02 * jax.random.normal(key, shape, jnp.float32)


def setup_inputs(seed: int = 0) -> dict:
    key = jax.random.key(seed)
    ks = jax.random.split(key, 26)
    d = D_MODEL
    offset = jax.random.randint(ks[2], (BATCH, 1), 0, MAX_STREAM_CHUNKS, dtype=jnp.int32) * CHUNK
    positions = offset + jnp.arange(SEQ, dtype=jnp.int32)[None, :]
    return {
        'x': _normal(ks[0], (BATCH, SEQ, d), 1.0),
        'mem': _normal(ks[1], (BATCH, N_MEM, d), 1.0),
        'positions': positions,
        'norm_mix_g': _gain(ks[3], (DEPTH, d)),
        'norm_mem_q_g': _gain(ks[4], (DEPTH, d)),
        'norm_mem_kv_g': _gain(ks[5], (DEPTH, d)),
        'norm_ffn_g': _gain(ks[6], (DEPTH, d)),
        'final_norm_g': _gain(ks[7], (d,)),
        'ab_w_in': _normal(ks[8], (N_EVEN, d, AB_IN), d ** -0.5),
        'a_v_norm_g': _gain(ks[9], (N_EVEN, A_WIDTH)),
        'a_w_s': _normal(ks[10], (N_EVEN, A_GROUPS, GMLP_BLOCK, GMLP_BLOCK), GMLP_BLOCK ** -0.5),
        'a_b_s': _gain(ks[11], (N_EVEN, A_GROUPS, GMLP_BLOCK)),
        'b_conv_w': _normal(ks[12], (N_EVEN, CONV_WIDTH, B_WIDTH), CONV_WIDTH ** -0.5),
        'ab_w_out': _normal(ks[13], (N_EVEN, A_WIDTH + B_WIDTH, d), (A_WIDTH + B_WIDTH) ** -0.5),
        'c_w_in': _normal(ks[14], (N_ODD, d, C_IN), d ** -0.5),
        'c_q_norm_g': _gain(ks[15], (N_ODD, C_Q_RANK)),
        'c_kv_norm_g': _gain(ks[16], (N_ODD, C_KV_RANK)),
        'c_w_uq': _normal(ks[17], (N_ODD, C_Q_RANK, C_HEADS * (C_NOPE + C_ROPE)), C_Q_RANK ** -0.5),
        'c_w_ukv': _normal(ks[18], (N_ODD, C_KV_RANK, C_HEADS * (C_NOPE + C_V)), C_KV_RANK ** -0.5),
        'c_w_out': _normal(ks[19], (N_ODD, C_HEADS * C_V, d), (C_HEADS * C_V) ** -0.5),
        'm_wq': _normal(ks[20], (DEPTH, d, d), d ** -0.5),
        'm_wk': _normal(ks[21], (DEPTH, d, d), d ** -0.5),
        'm_wv': _normal(ks[22], (DEPTH, d, d), d ** -0.5),
        'm_wo': _normal(ks[23], (DEPTH, d, d), d ** -0.5),
        'f_w1': _normal(ks[24], (DEPTH, d, D_FF), d ** -0.5),
        'f_w2': _normal(ks[25], (DEPTH, D_FF, d), D_FF ** -0.5),
    }


def reference(x, mem, positions, norm_mix_g, norm_mem_q_g, norm_mem_kv_g, norm_ffn_g,
              final_norm_g, ab_w_in, a_v_norm_g, a_w_s, a_b_s, b_conv_w, ab_w_out,
              c_w_in, c_q_norm_g, c_kv_norm_g, c_w_uq, c_w_ukv, c_w_out,
              m_wq, m_wk, m_wv, m_wo, f_w1, f_w2):
    for layer in range(DEPTH):
        xn = rmsnorm(x, norm_mix_g[layer])
        if layer % 2 == 0:
            e = layer // 2
            x = x + mixer_gmlp_conv(xn, ab_w_in[e], a_v_norm_g[e], a_w_s[e], a_b_s[e],
                                    b_conv_w[e], ab_w_out[e])
        else:
            o = layer // 2
            x = x + mixer_mla(xn, positions, c_w_in[o], c_q_norm_g[o], c_kv_norm_g[o],
                              c_w_uq[o], c_w_ukv[o], c_w_out[o])
        x = x + mem_cross_attention(rmsnorm(x, norm_mem_q_g[layer]),
                                    rmsnorm(mem, norm_mem_kv_g[layer]),
                                    m_wq[layer], m_wk[layer], m_wv[layer], m_wo[layer])
        x = x + squared_relu_mlp(rmsnorm(x, norm_ffn_g[layer]), f_w1[layer], f_w2[layer])
    return rmsnorm(x, final_norm_g)
```

```python
import functools

import jax
import jax.numpy as jnp
from jax import lax
from jax.experimental import pallas as pl
from jax.experimental.pallas import tpu as pltpu

F32 = jnp.float32
BF16 = jnp.bfloat16

D_MODEL = 2048
SEQ = 2048
CHUNK = 64
N_MEM = 256
RMS_EPS = 1e-6
GMLP_BLOCK = 128
A_WIDTH = 1024
A_GROUPS = 8
B_WIDTH = 1024
C_HEADS = 16
C_NOPE = 128
C_ROPE = 64
C_V = 128
C_Q_RANK = 512
C_KV_RANK = 256
ROPE_THETA = 10000.0
MEM_HEADS = 4
MEM_HEAD_DIM = 512
D_FF = 8192
LANES = 128
MIB = 1024 * 1024


def _params(semantics, vmem_mib):
    return pltpu.CompilerParams(dimension_semantics=semantics, vmem_limit_bytes=vmem_mib * MIB)


def _rms(x, g):
    ms = jnp.mean(x * x, axis=-1, keepdims=True)
    return x * lax.rsqrt(ms + RMS_EPS) * g


def _norm_rows_into(x_ref, g_ref, xn_ref, rows=256):
    def body(r, c):
        sl = pl.ds(pl.multiple_of(r * rows, rows), rows)
        xn_ref[sl, :] = _rms(x_ref[sl, :], g_ref[...]).astype(BF16)
        return c
    lax.fori_loop(0, x_ref.shape[0] // rows, body, 0)


def _norm_mm_kernel(x_ref, g_ref, w_ref, o_ref, xn_ref):
    @pl.when(pl.program_id(1) == 0)
    def _():
        _norm_rows_into(x_ref, g_ref, xn_ref)
    o_ref[...] = jnp.dot(xn_ref[...], w_ref[...], preferred_element_type=F32).astype(o_ref.dtype)


def norm_matmul(x, g, w, *, tm, tn, out_dtype, name):
    m, k = x.shape
    n = w.shape[1]
    return pl.pallas_call(
        _norm_mm_kernel,
        grid=(m // tm, n // tn),
        in_specs=[pl.BlockSpec((tm, k), lambda i, j: (i, 0)),
                  pl.BlockSpec((1, k), lambda i, j: (0, 0)),
                  pl.BlockSpec((k, tn), lambda i, j: (0, j))],
        out_specs=pl.BlockSpec((tm, tn), lambda i, j: (i, j)),
        out_shape=jax.ShapeDtypeStruct((m, n), out_dtype),
        scratch_shapes=[pltpu.VMEM((tm, k), BF16)],
        compiler_params=_params(("parallel", "arbitrary"), 48),
        name=name,
    )(x, g.reshape(1, k), w)


def _mm_res_kernel(a_ref, w_ref, r_ref, o_ref):
    o_ref[...] = r_ref[...] + jnp.dot(a_ref[...], w_ref[...], preferred_element_type=F32)


def matmul_residual(a, w, res, *, tm, tn, name):
    m, k = a.shape
    n = w.shape[1]
    return pl.pallas_call(
        _mm_res_kernel,
        grid=(m // tm, n // tn),
        in_specs=[pl.BlockSpec((tm, k), lambda i, j: (i, 0)),
                  pl.BlockSpec((k, tn), lambda i, j: (0, j)),
                  pl.BlockSpec((tm, tn), lambda i, j: (i, j))],
        out_specs=pl.BlockSpec((tm, tn), lambda i, j: (i, j)),
        out_shape=jax.ShapeDtypeStruct((m, n), F32),
        compiler_params=_params(("parallel", "parallel"), 48),
        name=name,
    )(a, w, res)


def _ffn_kernel(x_ref, g_ref, w1_ref, w2_ref, gf_ref, o_ref, xn_ref, *, final_norm):
    f = pl.program_id(1)

    @pl.when(f == 0)
    def _():
        _norm_rows_into(x_ref, g_ref, xn_ref)
        o_ref[...] = x_ref[...]

    h = jnp.dot(xn_ref[...], w1_ref[...], preferred_element_type=F32)
    h = jnp.maximum(h, 0.0)
    h = (h * h).astype(BF16)
    o_ref[...] += jnp.dot(h, w2_ref[...], preferred_element_type=F32)

    if final_norm:
        @pl.when(f == pl.num_programs(1) - 1)
        def _():
            rows = 256

            def body(r, c):
                sl = pl.ds(pl.multiple_of(r * rows, rows), rows)
                o_ref[sl, :] = _rms(o_ref[sl, :], gf_ref[...])
                return c
            lax.fori_loop(0, o_ref.shape[0] // rows, body, 0)


def ffn(x, g, w1, w2, gf, *, tm, tf, final_norm, name):
    m, d = x.shape
    dff = w1.shape[1]
    return pl.pallas_call(
        functools.partial(_ffn_kernel, final_norm=final_norm),
        grid=(m // tm, dff // tf),
        in_specs=[pl.BlockSpec((tm, d), lambda i, f: (i, 0)),
                  pl.BlockSpec((1, d), lambda i, f: (0, 0)),
                  pl.BlockSpec((d, tf), lambda i, f: (0, f)),
                  pl.BlockSpec((tf, d), lambda i, f: (f, 0)),
                  pl.BlockSpec((1, d), lambda i, f: (0, 0))],
        out_specs=pl.BlockSpec((tm, d), lambda i, f: (i, 0)),
        out_shape=jax.ShapeDtypeStruct((m, d), F32),
        scratch_shapes=[pltpu.VMEM((tm, d), BF16)],
        compiler_params=_params(("parallel", "arbitrary"), 52),
        name=name,
    )(x, g.reshape(1, d), w1, w2, gf.reshape(1, d))


def _gelu(x):
    c = 0.7978845608028654
    return 0.5 * x * (1.0 + jnp.tanh(c * (x + 0.044715 * (x * x * x))))


def _gmlp_conv_kernel(z_ref, hc_ref, hh_ref, vg_ref, ws_ref, bs_ref, cw_ref, y_ref, *, tm):
    i = pl.program_id(0)
    nblk = tm // GMLP_BLOCK

    ri = lax.broadcasted_iota(jnp.int32, (GMLP_BLOCK, GMLP_BLOCK), 0) // CHUNK
    ci = lax.broadcasted_iota(jnp.int32, (GMLP_BLOCK, GMLP_BLOCK), 1) // CHUNK
    keep = ci <= ri

    for blk in range(nblk):
        r0 = blk * GMLP_BLOCK
        for g in range(A_GROUPS):
            c0 = g * LANES
            u = _gelu(z_ref[r0:r0 + GMLP_BLOCK, c0:c0 + LANES].astype(F32))
            v = _gelu(z_ref[r0:r0 + GMLP_BLOCK, A_WIDTH + c0:A_WIDTH + c0 + LANES].astype(F32))
            vn = _rms(v, vg_ref[:, c0:c0 + LANES]).astype(BF16)
            w = jnp.where(keep, ws_ref[g], 0.0).astype(BF16)
            mixed = jnp.dot(w, vn, preferred_element_type=F32) + bs_ref[:, g:g + 1]
            y_ref[r0:r0 + GMLP_BLOCK, c0:c0 + LANES] = (u * mixed).astype(y_ref.dtype)

    halo_on = (i * tm) % SEQ != 0
    w0 = cw_ref[0:1, :]
    w1 = cw_ref[1:2, :]
    w2 = cw_ref[2:3, :]
    zc = z_ref[:, 3 * 1024:4 * 1024].astype(F32) * z_ref[:, 4 * 1024:5 * 1024].astype(F32)
    hz = hc_ref[...].astype(F32) * hh_ref[...].astype(F32)
    hz = jnp.where(halo_on, hz, 0.0)
    big = jnp.concatenate([hz, zc], axis=0)
    hrows = hz.shape[0]
    s1 = pltpu.roll(big, 1, axis=0)[hrows:]
    s2 = pltpu.roll(big, 2, axis=0)[hrows:]
    conv = w2 * zc + w1 * s1 + w0 * s2
    bg = z_ref[:, 2 * 1024:3 * 1024].astype(F32)
    y_ref[:, A_WIDTH:] = (bg * conv).astype(y_ref.dtype)


def gmlp_conv(z, v_norm_g, w_s, b_s, conv_w, *, tm, name):
    t = z.shape[0]
    hrows = 16
    hb = tm // hrows
    return pl.pallas_call(
        functools.partial(_gmlp_conv_kernel, tm=tm),
        grid=(t // tm,),
        in_specs=[pl.BlockSpec((tm, 5 * 1024), lambda i: (i, 0)),
                  pl.BlockSpec((hrows, 1024), lambda i: (jnp.maximum(i * hb - 1, 0), 3)),
                  pl.BlockSpec((hrows, 1024), lambda i: (jnp.maximum(i * hb - 1, 0), 4)),
                  pl.BlockSpec((1, A_WIDTH), lambda i: (0, 0)),
                  pl.BlockSpec((A_GROUPS, GMLP_BLOCK, GMLP_BLOCK), lambda i: (0, 0, 0)),
                  pl.BlockSpec((GMLP_BLOCK, A_GROUPS), lambda i: (0, 0)),
                  pl.BlockSpec((3, B_WIDTH), lambda i: (0, 0))],
        out_specs=pl.BlockSpec((tm, D_MODEL), lambda i: (i, 0)),
        out_shape=jax.ShapeDtypeStruct((t, D_MODEL), BF16),
        compiler_params=_params(("parallel",), 40),
        name=name,
    )(z, z, z, v_norm_g.reshape(1, A_WIDTH), w_s, b_s.T, conv_w)


def _rope_table_kernel(pos_ref, inv_ref, sign_ref, cs_ref, sn_ref):
    ang = pos_ref[...].astype(F32) * inv_ref[...]
    cs_ref[...] = jnp.cos(ang)
    sn_ref[...] = jnp.sin(ang) * sign_ref[...]


def rope_tables(positions, *, tm=2048):
    t = positions.size
    half = C_ROPE // 2
    inv = ROPE_THETA ** (-jnp.arange(0, C_ROPE, 2, dtype=F32) / C_ROPE)
    inv128 = jnp.tile(inv, LANES // half).reshape(1, LANES)
    sign = jnp.tile(jnp.concatenate([-jnp.ones((half,), F32), jnp.ones((half,), F32)]), 2).reshape(1, LANES)
    return pl.pallas_call(
        _rope_table_kernel,
        grid=(t // tm,),
        in_specs=[pl.BlockSpec((tm, 1), lambda i: (i, 0)),
                  pl.BlockSpec((1, LANES), lambda i: (0, 0)),
                  pl.BlockSpec((1, LANES), lambda i: (0, 0))],
        out_specs=[pl.BlockSpec((tm, LANES), lambda i: (i, 0)),
                   pl.BlockSpec((tm, LANES), lambda i: (i, 0))],
        out_shape=[jax.ShapeDtypeStruct((t, LANES), F32)] * 2,
        compiler_params=_params(("parallel",), 32),
        name="rope_tables",
    )(positions.reshape(t, 1), inv128, sign)


def _swap_halves(x):
    lane = lax.broadcasted_iota(jnp.int32, x.shape, 1)
    first = (lane % C_ROPE) < (C_ROPE // 2)
    return jnp.where(first, pltpu.roll(x, LANES - C_ROPE // 2, axis=1), pltpu.roll(x, C_ROPE // 2, axis=1))


def _mla_prep_kernel(x_ref, g_ref, win_ref, qg_ref, kvg_ref, wuq_ref, wukv_ref, cs_ref, sn_ref,
                     qn_ref, qr_ref, kn_ref, v_ref, kr_ref):
    xn = _rms(x_ref[...], g_ref[...]).astype(BF16)
    z = jnp.dot(xn, win_ref[...], preferred_element_type=F32)
    cq = _rms(z[:, :C_Q_RANK], qg_ref[...]).astype(BF16)
    ckv = _rms(z[:, C_Q_RANK:C_Q_RANK + C_KV_RANK], kvg_ref[...]).astype(BF16)
    cs = cs_ref[...]
    sn = sn_ref[...]

    kr2 = z[:, C_Q_RANK + C_KV_RANK:]
    kr2 = kr2 * cs + _swap_halves(kr2) * sn
    lane = lax.broadcasted_iota(jnp.int32, kr2.shape, 1)
    kr_ref[0] = jnp.where(lane < C_ROPE, kr2, 0.0).astype(BF16)
    kr_ref[1] = jnp.where(lane >= C_ROPE, kr2, 0.0).astype(BF16)

    q = jnp.dot(cq, wuq_ref[...], preferred_element_type=F32)
    for h in range(C_HEADS):
        qn_ref[h] = q[:, h * C_NOPE:(h + 1) * C_NOPE].astype(BF16)
    for p in range(C_HEADS // 2):
        base = C_HEADS * C_NOPE + p * LANES
        xr = q[:, base:base + LANES]
        qr_ref[p] = (xr * cs + _swap_halves(xr) * sn).astype(BF16)

    kv = jnp.dot(ckv, wukv_ref[...], preferred_element_type=F32)
    for h in range(C_HEADS):
        kn_ref[h] = kv[:, h * C_NOPE:(h + 1) * C_NOPE].astype(BF16)
        v_ref[h] = kv[:, C_HEADS * C_NOPE + h * C_V:C_HEADS * C_NOPE + (h + 1) * C_V].astype(BF16)


def mla_prep(x, g, w_in, q_g, kv_g, w_uq, w_ukv, cs, sn, *, tm, name):
    t, d = x.shape
    nin = w_in.shape[1]
    full = lambda shape: pl.BlockSpec(shape, lambda i: (0,) * len(shape))
    hspec = lambda n: pl.BlockSpec((n, tm, LANES), lambda i: (0, i, 0))
    hshape = lambda n: jax.ShapeDtypeStruct((n, t, LANES), BF16)
    return pl.pallas_call(
        _mla_prep_kernel,
        grid=(t // tm,),
        in_specs=[pl.BlockSpec((tm, d), lambda i: (i, 0)),
                  full((1, d)), full((d, nin)), full((1, C_Q_RANK)), full((1, C_KV_RANK)),
                  full(w_uq.shape), full(w_ukv.shape),
                  pl.BlockSpec((tm, LANES), lambda i: (i, 0)),
                  pl.BlockSpec((tm, LANES), lambda i: (i, 0))],
        out_specs=[hspec(C_HEADS), hspec(C_HEADS // 2), hspec(C_HEADS), hspec(C_HEADS), hspec(2)],
        out_shape=[hshape(C_HEADS), hshape(C_HEADS // 2), hshape(C_HEADS), hshape(C_HEADS), hshape(2)],
        compiler_params=_params(("parallel",), 48),
        name=name,
    )(x, g.reshape(1, d), w_in, q_g.reshape(1, -1), kv_g.reshape(1, -1), w_uq, w_ukv, cs, sn)


def _mla_attn_kernel(qn_ref, qr_ref, kn_ref, v_ref, kr_ref, o_ref, *, tq):
    scale = (C_NOPE + C_ROPE) ** -0.5
    nq = SEQ // tq
    rq = lax.broadcasted_iota(jnp.int32, (tq, tq), 0) // CHUNK
    ck = lax.broadcasted_iota(jnp.int32, (tq, tq), 1) // CHUNK
    diag_keep = ck <= rq
    nt = (((1,), (1,)), ((), ()))

    for hh in range(2):
        for t in range(nq):
            q = jnp.concatenate([qn_ref[hh, t * tq:(t + 1) * tq, :], qr_ref[0, t * tq:(t + 1) * tq, :]], axis=1)

            def scores(k0):
                k = jnp.concatenate([kn_ref[hh, pl.ds(k0, tq), :], kr_ref[hh, pl.ds(k0, tq), :]], axis=1)
                return lax.dot_general(q, k, nt, preferred_element_type=F32) * scale

            def update(carry, s, k0):
                m, l, acc = carry
                m_new = jnp.maximum(m, jnp.max(s, axis=-1, keepdims=True))
                alpha = jnp.exp(m - m_new)
                p = jnp.exp(s - m_new)
                l_new = alpha * l + jnp.sum(p, axis=-1, keepdims=True)
                pv = jnp.dot(p.astype(BF16), v_ref[hh, pl.ds(k0, tq), :], preferred_element_type=F32)
                return m_new, l_new, alpha * acc + pv

            def body(j, carry):
                k0 = pl.multiple_of(j * tq, tq)
                return update(carry, scores(k0), k0)

            init = (jnp.full((tq, 1), -jnp.inf, F32), jnp.zeros((tq, 1), F32), jnp.zeros((tq, C_V), F32))
            carry = lax.fori_loop(0, t, body, init)
            s = jnp.where(diag_keep, scores(t * tq), -jnp.inf)
            m, l, acc = update(carry, s, t * tq)
            o_ref[t * tq:(t + 1) * tq, hh * C_V:(hh + 1) * C_V] = (acc / l).astype(o_ref.dtype)


def mla_attention(qn, qr, kn, v, kr, *, bsz, tq, name):
    t = qn.shape[1]
    return pl.pallas_call(
        functools.partial(_mla_attn_kernel, tq=tq),
        grid=(bsz, C_HEADS // 2),
        in_specs=[pl.BlockSpec((2, SEQ, LANES), lambda b, p: (p, b, 0)),
                  pl.BlockSpec((1, SEQ, LANES), lambda b, p: (p, b, 0)),
                  pl.BlockSpec((2, SEQ, LANES), lambda b, p: (p, b, 0)),
                  pl.BlockSpec((2, SEQ, LANES), lambda b, p: (p, b, 0)),
                  pl.BlockSpec((2, SEQ, LANES), lambda b, p: (0, b, 0))],
        out_specs=pl.BlockSpec((SEQ, 2 * C_V), lambda b, p: (b, p)),
        out_shape=jax.ShapeDtypeStruct((t, C_HEADS * C_V), BF16),
        compiler_params=_params(("parallel", "parallel"), 32),
        name=name,
    )(qn, qr, kn, v, kr)


def _mem_attn_kernel(q_ref, k_ref, v_ref, o_ref):
    scale = MEM_HEAD_DIM ** -0.5
    nt = (((1,), (1,)), ((), ()))
    for h in range(MEM_HEADS):
        c0 = h * MEM_HEAD_DIM
        s = lax.dot_general(q_ref[:, c0:c0 + MEM_HEAD_DIM], k_ref[:, c0:c0 + MEM_HEAD_DIM], nt,
                            preferred_element_type=F32) * scale
        m = jnp.max(s, axis=-1, keepdims=True)
        p = jnp.exp(s - m)
        l = jnp.sum(p, axis=-1, keepdims=True)
        o = jnp.dot(p.astype(BF16), v_ref[:, c0:c0 + MEM_HEAD_DIM], preferred_element_type=F32)
        o_ref[:, c0:c0 + MEM_HEAD_DIM] = (o / l).astype(o_ref.dtype)


def mem_attention(q, k, v, *, tq, name):
    t, d = q.shape
    per_b = SEQ // tq
    return pl.pallas_call(
        _mem_attn_kernel,
        grid=(t // tq,),
        in_specs=[pl.BlockSpec((tq, d), lambda i: (i, 0)),
                  pl.BlockSpec((N_MEM, d), lambda i: (i // per_b, 0)),
                  pl.BlockSpec((N_MEM, d), lambda i: (i // per_b, 0))],
        out_specs=pl.BlockSpec((tq, d), lambda i: (i, 0)),
        out_shape=jax.ShapeDtypeStruct((t, d), BF16),
        compiler_params=_params(("parallel",), 40),
        name=name,
    )(q, k, v)


def kernel(x, mem, positions, norm_mix_g, norm_mem_q_g, norm_mem_kv_g, norm_ffn_g, final_norm_g, ab_w_in, a_v_norm_g, a_w_s, a_b_s, b_conv_w, ab_w_out, c_w_in, c_q_norm_g, c_kv_norm_g, c_w_uq, c_w_ukv, c_w_out, m_wq, m_wk, m_wv, m_wo, f_w1, f_w2):
    bsz, seq, d = x.shape
    depth = norm_mix_g.shape[0]
    t = bsz * seq
    xf = x.reshape(t, d)
    memf = mem.reshape(bsz * N_MEM, d)
    bf = lambda a: a.astype(BF16)

    cs, sn = rope_tables(positions)

    for layer in range(depth):
        if layer % 2 == 0:
            e = layer // 2
            z = norm_matmul(xf, norm_mix_g[layer], bf(ab_w_in[e]), tm=1024, tn=1024, out_dtype=BF16,
                            name=f"ab_in_{layer}")
            y = gmlp_conv(z, a_v_norm_g[e], a_w_s[e], a_b_s[e], b_conv_w[e], tm=256, name=f"gmlp_conv_{layer}")
            xf = matmul_residual(y, bf(ab_w_out[e]), xf, tm=1024, tn=1024, name=f"ab_out_{layer}")
        else:
            o = layer // 2
            w_in = c_w_in[o]
            w_in = jnp.concatenate([w_in, w_in[:, C_Q_RANK + C_KV_RANK:]], axis=1)
            wuq = c_w_uq[o].reshape(C_Q_RANK, C_HEADS, C_NOPE + C_ROPE)
            wuq = jnp.concatenate([wuq[:, :, :C_NOPE].reshape(C_Q_RANK, -1),
                                   wuq[:, :, C_NOPE:].reshape(C_Q_RANK, -1)], axis=1)
            wukv = c_w_ukv[o].reshape(C_KV_RANK, C_HEADS, C_NOPE + C_V)
            wukv = jnp.concatenate([wukv[:, :, :C_NOPE].reshape(C_KV_RANK, -1),
                                    wukv[:, :, C_NOPE:].reshape(C_KV_RANK, -1)], axis=1)
            qn, qr, kn, v, kr = mla_prep(xf, norm_mix_g[layer], bf(w_in), c_q_norm_g[o], c_kv_norm_g[o],
                                         bf(wuq), bf(wukv), cs, sn, tm=512, name=f"mla_prep_{layer}")
            att = mla_attention(qn, qr, kn, v, kr, bsz=bsz, tq=256, name=f"mla_attn_{layer}")
            xf = matmul_residual(att, bf(c_w_out[o]), xf, tm=1024, tn=1024, name=f"c_out_{layer}")

        q = norm_matmul(xf, norm_mem_q_g[layer], bf(m_wq[layer]), tm=1024, tn=1024, out_dtype=BF16,
                        name=f"mem_q_{layer}")
        mk = norm_matmul(memf, norm_mem_kv_g[layer], bf(m_wk[layer]), tm=1024, tn=1024, out_dtype=BF16,
                         name=f"mem_k_{layer}")
        mv = norm_matmul(memf, norm_mem_kv_g[layer], bf(m_wv[layer]), tm=1024, tn=1024, out_dtype=BF16,
                         name=f"mem_v_{layer}")
        ma = mem_attention(q, mk, mv, tq=1024, name=f"mem_attn_{layer}")
        xf = matmul_residual(ma, bf(m_wo[layer]), xf, tm=1024, tn=1024, name=f"mem_out_{layer}")

        xf = ffn(xf, norm_ffn_g[layer], bf(f_w1[layer]), bf(f_w2[layer]), final_norm_g, tm=512, tf=1024,
                 final_norm=(layer == depth - 1), name=f"ffn_{layer}")

    return xf.reshape(bsz, seq, d)
```

```python
import functools

import jax
import jax.numpy as jnp
from jax import lax
from jax.experimental import pallas as pl
from jax.experimental.pallas import tpu as pltpu

F32 = jnp.float32
BF16 = jnp.bfloat16

D_MODEL = 2048
SEQ = 2048
CHUNK = 64
N_MEM = 256
RMS_EPS = 1e-6
GMLP_BLOCK = 128
A_WIDTH = 1024
A_GROUPS = 8
B_WIDTH = 1024
C_HEADS = 16
C_NOPE = 128
C_ROPE = 64
C_V = 128
C_Q_RANK = 512
C_KV_RANK = 256
ROPE_THETA = 10000.0
MEM_HEADS = 4
MEM_HEAD_DIM = 512
D_FF = 8192
LANES = 128
MIB = 1024 * 1024


def _params(semantics, vmem_mib):
    return pltpu.CompilerParams(dimension_semantics=semantics, vmem_limit_bytes=vmem_mib * MIB)


def _rms(x, g):
    ms = jnp.mean(x * x, axis=-1, keepdims=True)
    return x * lax.rsqrt(ms + RMS_EPS) * g


def _norm_rows_into(x_ref, g_ref, xn_ref, rows=256):
    def body(r, c):
        sl = pl.ds(pl.multiple_of(r * rows, rows), rows)
        xn_ref[sl, :] = _rms(x_ref[sl, :], g_ref[...]).astype(BF16)
        return c
    lax.fori_loop(0, x_ref.shape[0] // rows, body, 0)


NORM_CHUNK_ROWS = 256


def _norm_mm_kernel(x_ref, g_ref, w_ref, o_ref, *scratch, n_col_steps):
    tm = x_ref.shape[0]

    def first():
        for r in range(tm // NORM_CHUNK_ROWS):
            sl = slice(r * NORM_CHUNK_ROWS, (r + 1) * NORM_CHUNK_ROWS)
            xn = _rms(x_ref[sl, :], g_ref[...]).astype(BF16)
            if n_col_steps > 1:
                scratch[0][sl, :] = xn
            o_ref[sl, :] = jnp.dot(xn, w_ref[...], preferred_element_type=F32).astype(o_ref.dtype)

    if n_col_steps == 1:
        first()
    else:
        pl.when(pl.program_id(1) == 0)(first)

        @pl.when(pl.program_id(1) != 0)
        def _():
            o_ref[...] = jnp.dot(scratch[0][...], w_ref[...], preferred_element_type=F32).astype(o_ref.dtype)


def norm_matmul(x, g, w, widx, *, tm, tn, out_dtype, vmem_mib, name):
    m, k = x.shape
    n = w.shape[2]
    n_col_steps = n // tn
    w_mode = dict(pipeline_mode=pl.Buffered(1)) if n_col_steps == 1 else {}
    return pl.pallas_call(
        functools.partial(_norm_mm_kernel, n_col_steps=n_col_steps),
        grid=(m // tm, n_col_steps),
        in_specs=[pl.BlockSpec((tm, k), lambda i, j: (i, 0)),
                  pl.BlockSpec((1, k), lambda i, j: (0, 0)),
                  pl.BlockSpec((None, k, tn), lambda i, j: (widx, 0, j), **w_mode)],
        out_specs=pl.BlockSpec((tm, tn), lambda i, j: (i, j)),
        out_shape=jax.ShapeDtypeStruct((m, n), out_dtype),
        scratch_shapes=[pltpu.VMEM((tm, k), BF16)] if n_col_steps > 1 else [],
        compiler_params=_params(("parallel", "arbitrary"), vmem_mib),
        name=name,
    )(x, g.reshape(1, k), w)


def _mm_res_kernel(a_ref, w_ref, r_ref, o_ref):
    o_ref[...] = r_ref[...] + jnp.dot(a_ref[...], w_ref[...], preferred_element_type=F32)


def matmul_residual(a, w, widx, res, *, tm, name):
    m, k = a.shape
    n = w.shape[2]
    return pl.pallas_call(
        _mm_res_kernel,
        grid=(m // tm,),
        in_specs=[pl.BlockSpec((tm, k), lambda i: (i, 0)),
                  pl.BlockSpec((None, k, n), lambda i: (widx, 0, 0), pipeline_mode=pl.Buffered(1)),
                  pl.BlockSpec((tm, n), lambda i: (i, 0))],
        out_specs=pl.BlockSpec((tm, n), lambda i: (i, 0)),
        out_shape=jax.ShapeDtypeStruct((m, n), F32),
        compiler_params=_params(("parallel",), 40),
        name=name,
    )(a, w, res)


def _ffn_kernel(x_ref, g_ref, w1_ref, w2_ref, gf_ref, o_ref, xn_ref, *, final_norm):
    f = pl.program_id(1)

    @pl.when(f == 0)
    def _():
        _norm_rows_into(x_ref, g_ref, xn_ref)
        o_ref[...] = x_ref[...]

    h = jnp.dot(xn_ref[...], w1_ref[...], preferred_element_type=F32)
    h = jnp.maximum(h, 0.0)
    h = (h * h).astype(BF16)
    o_ref[...] += jnp.dot(h, w2_ref[...], preferred_element_type=F32)

    if final_norm:
        @pl.when(f == pl.num_programs(1) - 1)
        def _():
            rows = 256

            def body(r, c):
                sl = pl.ds(pl.multiple_of(r * rows, rows), rows)
                o_ref[sl, :] = _rms(o_ref[sl, :], gf_ref[...])
                return c
            lax.fori_loop(0, o_ref.shape[0] // rows, body, 0)


def ffn(x, g, w1, w2, widx, gf, *, tm, tf, final_norm, name):
    m, d = x.shape
    dff = w1.shape[2]
    return pl.pallas_call(
        functools.partial(_ffn_kernel, final_norm=final_norm),
        grid=(m // tm, dff // tf),
        in_specs=[pl.BlockSpec((tm, d), lambda i, f: (i, 0)),
                  pl.BlockSpec((1, d), lambda i, f: (0, 0)),
                  pl.BlockSpec((None, d, tf), lambda i, f: (widx, 0, f)),
                  pl.BlockSpec((None, tf, d), lambda i, f: (widx, f, 0)),
                  pl.BlockSpec((1, d), lambda i, f: (0, 0))],
        out_specs=pl.BlockSpec((tm, d), lambda i, f: (i, 0)),
        out_shape=jax.ShapeDtypeStruct((m, d), F32),
        scratch_shapes=[pltpu.VMEM((tm, d), BF16)],
        compiler_params=_params(("parallel", "arbitrary"), 52),
        name=name,
    )(x, g.reshape(1, d), w1, w2, gf.reshape(1, d))


def _gelu(x):
    c = 0.7978845608028654
    return 0.5 * x * (1.0 + jnp.tanh(c * (x + 0.044715 * (x * x * x))))


def _gmlp_conv_kernel(z_ref, hc_ref, hh_ref, vg_ref, ws_ref, bs_ref, cw_ref, y_ref, *, tm):
    i = pl.program_id(0)
    nblk = tm // GMLP_BLOCK

    ri = lax.broadcasted_iota(jnp.int32, (GMLP_BLOCK, GMLP_BLOCK), 0) // CHUNK
    ci = lax.broadcasted_iota(jnp.int32, (GMLP_BLOCK, GMLP_BLOCK), 1) // CHUNK
    keep = ci <= ri

    for blk in range(nblk):
        r0 = blk * GMLP_BLOCK
        for g in range(A_GROUPS):
            c0 = g * LANES
            u = _gelu(z_ref[r0:r0 + GMLP_BLOCK, c0:c0 + LANES].astype(F32))
            v = _gelu(z_ref[r0:r0 + GMLP_BLOCK, A_WIDTH + c0:A_WIDTH + c0 + LANES].astype(F32))
            vn = _rms(v, vg_ref[:, c0:c0 + LANES]).astype(BF16)
            w = jnp.where(keep, ws_ref[g], 0.0).astype(BF16)
            mixed = jnp.dot(w, vn, preferred_element_type=F32) + bs_ref[:, g:g + 1]
            y_ref[r0:r0 + GMLP_BLOCK, c0:c0 + LANES] = (u * mixed).astype(y_ref.dtype)

    halo_on = (i * tm) % SEQ != 0
    w0 = cw_ref[0:1, :]
    w1 = cw_ref[1:2, :]
    w2 = cw_ref[2:3, :]
    zc = z_ref[:, 3 * 1024:4 * 1024].astype(F32) * z_ref[:, 4 * 1024:5 * 1024].astype(F32)
    hz = hc_ref[...].astype(F32) * hh_ref[...].astype(F32)
    hz = jnp.where(halo_on, hz, 0.0)
    big = jnp.concatenate([hz, zc], axis=0)
    hrows = hz.shape[0]
    s1 = pltpu.roll(big, 1, axis=0)[hrows:]
    s2 = pltpu.roll(big, 2, axis=0)[hrows:]
    conv = w2 * zc + w1 * s1 + w0 * s2
    bg = z_ref[:, 2 * 1024:3 * 1024].astype(F32)
    y_ref[:, A_WIDTH:] = (bg * conv).astype(y_ref.dtype)


def gmlp_conv(z, v_norm_g, w_s, b_s, conv_w, *, tm, name):
    t = z.shape[0]
    hrows = 16
    hb = tm // hrows
    return pl.pallas_call(
        functools.partial(_gmlp_conv_kernel, tm=tm),
        grid=(t // tm,),
        in_specs=[pl.BlockSpec((tm, 5 * 1024), lambda i: (i, 0)),
                  pl.BlockSpec((hrows, 1024), lambda i: (jnp.maximum(i * hb - 1, 0), 3)),
                  pl.BlockSpec((hrows, 1024), lambda i: (jnp.maximum(i * hb - 1, 0), 4)),
                  pl.BlockSpec((1, A_WIDTH), lambda i: (0, 0)),
                  pl.BlockSpec((A_GROUPS, GMLP_BLOCK, GMLP_BLOCK), lambda i: (0, 0, 0)),
                  pl.BlockSpec((GMLP_BLOCK, A_GROUPS), lambda i: (0, 0)),
                  pl.BlockSpec((3, B_WIDTH), lambda i: (0, 0))],
        out_specs=pl.BlockSpec((tm, D_MODEL), lambda i: (i, 0)),
        out_shape=jax.ShapeDtypeStruct((t, D_MODEL), BF16),
        compiler_params=_params(("parallel",), 40),
        name=name,
    )(z, z, z, v_norm_g.reshape(1, A_WIDTH), w_s, b_s.T, conv_w)


def _rope_table_kernel(pos_ref, inv_ref, sign_ref, cs_ref, sn_ref):
    ang = pos_ref[...].astype(F32) * inv_ref[...]
    cs_ref[...] = jnp.cos(ang)
    sn_ref[...] = jnp.sin(ang) * sign_ref[...]


def rope_tables(positions, *, tm=2048):
    t = positions.size
    half = C_ROPE // 2
    inv = ROPE_THETA ** (-jnp.arange(0, C_ROPE, 2, dtype=F32) / C_ROPE)
    inv128 = jnp.tile(inv, LANES // half).reshape(1, LANES)
    sign = jnp.tile(jnp.concatenate([-jnp.ones((half,), F32), jnp.ones((half,), F32)]), 2).reshape(1, LANES)
    return pl.pallas_call(
        _rope_table_kernel,
        grid=(t // tm,),
        in_specs=[pl.BlockSpec((tm, 1), lambda i: (i, 0)),
                  pl.BlockSpec((1, LANES), lambda i: (0, 0)),
                  pl.BlockSpec((1, LANES), lambda i: (0, 0))],
        out_specs=[pl.BlockSpec((tm, LANES), lambda i: (i, 0)),
                   pl.BlockSpec((tm, LANES), lambda i: (i, 0))],
        out_shape=[jax.ShapeDtypeStruct((t, LANES), F32)] * 2,
        compiler_params=_params(("parallel",), 32),
        name="rope_tables",
    )(positions.reshape(t, 1), inv128, sign)


def _swap_halves(x):
    lane = lax.broadcasted_iota(jnp.int32, x.shape, 1)
    first = (lane % C_ROPE) < (C_ROPE // 2)
    return jnp.where(first, pltpu.roll(x, LANES - C_ROPE // 2, axis=1), pltpu.roll(x, C_ROPE // 2, axis=1))


def _mla_prep_kernel(x_ref, g_ref, win_ref, qg_ref, kvg_ref, wuq_ref, wukv_ref, cs_ref, sn_ref,
                     qn_ref, qr_ref, kn_ref, v_ref, kr_ref):
    xn = _rms(x_ref[...], g_ref[...]).astype(BF16)
    z = jnp.dot(xn, win_ref[...], preferred_element_type=F32)
    cq = _rms(z[:, :C_Q_RANK], qg_ref[...]).astype(BF16)
    ckv = _rms(z[:, C_Q_RANK:C_Q_RANK + C_KV_RANK], kvg_ref[...]).astype(BF16)
    cs = cs_ref[...]
    sn = sn_ref[...]

    kr2 = z[:, C_Q_RANK + C_KV_RANK:]
    kr2 = kr2 * cs + _swap_halves(kr2) * sn
    lane = lax.broadcasted_iota(jnp.int32, kr2.shape, 1)
    kr_ref[0] = jnp.where(lane < C_ROPE, kr2, 0.0).astype(BF16)
    kr_ref[1] = jnp.where(lane >= C_ROPE, kr2, 0.0).astype(BF16)

    q = jnp.dot(cq, wuq_ref[...], preferred_element_type=F32)
    for h in range(C_HEADS):
        qn_ref[h] = q[:, h * C_NOPE:(h + 1) * C_NOPE].astype(BF16)
    for p in range(C_HEADS // 2):
        base = C_HEADS * C_NOPE + p * LANES
        xr = q[:, base:base + LANES]
        qr_ref[p] = (xr * cs + _swap_halves(xr) * sn).astype(BF16)

    kv = jnp.dot(ckv, wukv_ref[...], preferred_element_type=F32)
    for h in range(C_HEADS):
        kn_ref[h] = kv[:, h * C_NOPE:(h + 1) * C_NOPE].astype(BF16)
        v_ref[h] = kv[:, C_HEADS * C_NOPE + h * C_V:C_HEADS * C_NOPE + (h + 1) * C_V].astype(BF16)


def mla_prep(x, g, w_in, q_g, kv_g, w_uq, w_ukv, widx, cs, sn, *, tm, name):
    t, d = x.shape
    full = lambda shape: pl.BlockSpec(shape, lambda i: (0,) * len(shape))
    layer = lambda w: pl.BlockSpec((None,) + w.shape[1:], lambda i: (widx, 0, 0))
    hspec = lambda n: pl.BlockSpec((n, tm, LANES), lambda i: (0, i, 0))
    hshape = lambda n: jax.ShapeDtypeStruct((n, t, LANES), BF16)
    return pl.pallas_call(
        _mla_prep_kernel,
        grid=(t // tm,),
        in_specs=[pl.BlockSpec((tm, d), lambda i: (i, 0)),
                  full((1, d)), layer(w_in), full((1, C_Q_RANK)), full((1, C_KV_RANK)),
                  layer(w_uq), layer(w_ukv),
                  pl.BlockSpec((tm, LANES), lambda i: (i, 0)),
                  pl.BlockSpec((tm, LANES), lambda i: (i, 0))],
        out_specs=[hspec(C_HEADS), hspec(C_HEADS // 2), hspec(C_HEADS), hspec(C_HEADS), hspec(2)],
        out_shape=[hshape(C_HEADS), hshape(C_HEADS // 2), hshape(C_HEADS), hshape(C_HEADS), hshape(2)],
        compiler_params=_params(("parallel",), 48),
        name=name,
    )(x, g.reshape(1, d), w_in, q_g.reshape(1, -1), kv_g.reshape(1, -1), w_uq, w_ukv, cs, sn)


def _mla_attn_kernel(qn_ref, qr_ref, kn_ref, v_ref, kr_ref, o_ref, *, tq):
    exp_mult = (C_NOPE + C_ROPE) ** -0.5 * 1.4426950408889634
    nq = SEQ // tq
    rq = lax.broadcasted_iota(jnp.int32, (tq, tq), 0) // CHUNK
    ck = lax.broadcasted_iota(jnp.int32, (tq, tq), 1) // CHUNK
    diag_keep = ck <= rq
    nt = (((1,), (1,)), ((), ()))

    def keys(hh, lo, hi):
        return jnp.concatenate([kn_ref[hh, lo:hi, :], kr_ref[hh, lo:hi, :]], axis=1)

    for hh in range(2):
        for t in range(nq):
            lo, hi = t * tq, (t + 1) * tq
            q = jnp.concatenate([qn_ref[hh, lo:hi, :], qr_ref[0, lo:hi, :]], axis=1)
            s_d = lax.dot_general(q, keys(hh, lo, hi), nt, preferred_element_type=F32)
            s_d = jnp.where(diag_keep, s_d, -jnp.inf)
            m = jnp.max(s_d, axis=-1, keepdims=True)
            if t > 0:
                s_p = lax.dot_general(q, keys(hh, 0, lo), nt, preferred_element_type=F32)
                m = jnp.maximum(m, jnp.max(s_p, axis=-1, keepdims=True))
            p_d = jnp.exp2((s_d - m) * exp_mult)
            l = jnp.sum(p_d, axis=-1, keepdims=True)
            acc = jnp.dot(p_d.astype(BF16), v_ref[hh, lo:hi, :], preferred_element_type=F32)
            if t > 0:
                p_p = jnp.exp2((s_p - m) * exp_mult)
                l = l + jnp.sum(p_p, axis=-1, keepdims=True)
                acc = acc + jnp.dot(p_p.astype(BF16), v_ref[hh, 0:lo, :], preferred_element_type=F32)
            o_ref[lo:hi, hh * C_V:(hh + 1) * C_V] = (acc / l).astype(o_ref.dtype)


def mla_attention(qn, qr, kn, v, kr, *, bsz, tq, name):
    t = qn.shape[1]
    return pl.pallas_call(
        functools.partial(_mla_attn_kernel, tq=tq),
        grid=(bsz, C_HEADS // 2),
        in_specs=[pl.BlockSpec((2, SEQ, LANES), lambda b, p: (p, b, 0)),
                  pl.BlockSpec((1, SEQ, LANES), lambda b, p: (p, b, 0)),
                  pl.BlockSpec((2, SEQ, LANES), lambda b, p: (p, b, 0)),
                  pl.BlockSpec((2, SEQ, LANES), lambda b, p: (p, b, 0)),
                  pl.BlockSpec((2, SEQ, LANES), lambda b, p: (0, b, 0))],
        out_specs=pl.BlockSpec((SEQ, 2 * C_V), lambda b, p: (b, p)),
        out_shape=jax.ShapeDtypeStruct((t, C_HEADS * C_V), BF16),
        compiler_params=_params(("parallel", "parallel"), 32),
        name=name,
    )(qn, qr, kn, v, kr)


def _mem_attn_kernel(q_ref, k_ref, v_ref, o_ref):
    scale = MEM_HEAD_DIM ** -0.5
    nt = (((1,), (1,)), ((), ()))
    for h in range(MEM_HEADS):
        c0 = h * MEM_HEAD_DIM
        s = lax.dot_general(q_ref[:, c0:c0 + MEM_HEAD_DIM], k_ref[:, c0:c0 + MEM_HEAD_DIM], nt,
                            preferred_element_type=F32) * scale
        m = jnp.max(s, axis=-1, keepdims=True)
        p = jnp.exp(s - m)
        l = jnp.sum(p, axis=-1, keepdims=True)
        o = jnp.dot(p.astype(BF16), v_ref[:, c0:c0 + MEM_HEAD_DIM], preferred_element_type=F32)
        o_ref[:, c0:c0 + MEM_HEAD_DIM] = (o / l).astype(o_ref.dtype)


def mem_attention(q, k, v, *, tq, name):
    t, d = q.shape
    per_b = SEQ // tq
    return pl.pallas_call(
        _mem_attn_kernel,
        grid=(t // tq,),
        in_specs=[pl.BlockSpec((tq, d), lambda i: (i, 0)),
                  pl.BlockSpec((N_MEM, d), lambda i: (i // per_b, 0)),
                  pl.BlockSpec((N_MEM, d), lambda i: (i // per_b, 0))],
        out_specs=pl.BlockSpec((tq, d), lambda i: (i, 0)),
        out_shape=jax.ShapeDtypeStruct((t, d), BF16),
        compiler_params=_params(("parallel",), 40),
        name=name,
    )(q, k, v)


def kernel(x, mem, positions, norm_mix_g, norm_mem_q_g, norm_mem_kv_g, norm_ffn_g, final_norm_g, ab_w_in, a_v_norm_g, a_w_s, a_b_s, b_conv_w, ab_w_out, c_w_in, c_q_norm_g, c_kv_norm_g, c_w_uq, c_w_ukv, c_w_out, m_wq, m_wk, m_wv, m_wo, f_w1, f_w2):
    bsz, seq, d = x.shape
    depth = norm_mix_g.shape[0]
    t = bsz * seq
    xf = x.reshape(t, d)
    memf = mem.reshape(bsz * N_MEM, d)
    bf = lambda a: a.astype(BF16)

    w_ab_in, w_ab_out, w_c_out = bf(ab_w_in), bf(ab_w_out), bf(c_w_out)
    w_mq, w_mk, w_mv, w_mo = bf(m_wq), bf(m_wk), bf(m_wv), bf(m_wo)
    w_f1, w_f2 = bf(f_w1), bf(f_w2)
    n_odd = c_w_in.shape[0]
    w_c_in = bf(jnp.concatenate([c_w_in, c_w_in[:, :, C_Q_RANK + C_KV_RANK:]], axis=2))
    wuq = c_w_uq.reshape(n_odd, C_Q_RANK, C_HEADS, C_NOPE + C_ROPE)
    w_uq = bf(jnp.concatenate([wuq[..., :C_NOPE].reshape(n_odd, C_Q_RANK, -1),
                               wuq[..., C_NOPE:].reshape(n_odd, C_Q_RANK, -1)], axis=2))
    wukv = c_w_ukv.reshape(n_odd, C_KV_RANK, C_HEADS, C_NOPE + C_V)
    w_ukv = bf(jnp.concatenate([wukv[..., :C_NOPE].reshape(n_odd, C_KV_RANK, -1),
                                wukv[..., C_NOPE:].reshape(n_odd, C_KV_RANK, -1)], axis=2))

    cs, sn = rope_tables(positions)

    for layer in range(depth):
        if layer % 2 == 0:
            e = layer // 2
            z = norm_matmul(xf, norm_mix_g[layer], w_ab_in, e, tm=1024, tn=1024, out_dtype=BF16, vmem_mib=48,
                            name=f"ab_in_{layer}")
            y = gmlp_conv(z, a_v_norm_g[e], a_w_s[e], a_b_s[e], b_conv_w[e], tm=256, name=f"gmlp_conv_{layer}")
            xf = matmul_residual(y, w_ab_out, e, xf, tm=512, name=f"ab_out_{layer}")
        else:
            o = layer // 2
            qn, qr, kn, v, kr = mla_prep(xf, norm_mix_g[layer], w_c_in, c_q_norm_g[o], c_kv_norm_g[o],
                                         w_uq, w_ukv, o, cs, sn, tm=512, name=f"mla_prep_{layer}")
            att = mla_attention(qn, qr, kn, v, kr, bsz=bsz, tq=256, name=f"mla_attn_{layer}")
            xf = matmul_residual(att, w_c_out, o, xf, tm=512, name=f"c_out_{layer}")

        q = norm_matmul(xf, norm_mem_q_g[layer], w_mq, layer, tm=512, tn=d, out_dtype=BF16, vmem_mib=32,
                        name=f"mem_q_{layer}")
        mk = norm_matmul(memf, norm_mem_kv_g[layer], w_mk, layer, tm=512, tn=d, out_dtype=BF16, vmem_mib=32,
                         name=f"mem_k_{layer}")
        mv = norm_matmul(memf, norm_mem_kv_g[layer], w_mv, layer, tm=512, tn=d, out_dtype=BF16, vmem_mib=32,
                         name=f"mem_v_{layer}")
        ma = mem_attention(q, mk, mv, tq=1024, name=f"mem_attn_{layer}")
        xf = matmul_residual(ma, w_mo, layer, xf, tm=512, name=f"mem_out_{layer}")

        xf = ffn(xf, norm_ffn_g[layer], w_f1, w_f2, layer, final_norm_g, tm=512, tf=1024,
                 final_norm=(layer == depth - 1), name=f"ffn_{layer}")

    return xf.reshape(bsz, seq, d)
```

```python
import functools

import jax
import jax.numpy as jnp
from jax import lax
from jax.experimental import pallas as pl
from jax.experimental.pallas import tpu as pltpu

F32 = jnp.float32
BF16 = jnp.bfloat16

D_MODEL = 2048
SEQ = 2048
CHUNK = 64
N_MEM = 256
RMS_EPS = 1e-6
GMLP_BLOCK = 128
A_WIDTH = 1024
A_GROUPS = 8
B_WIDTH = 1024
C_HEADS = 16
C_NOPE = 128
C_ROPE = 64
C_V = 128
C_Q_RANK = 512
C_KV_RANK = 256
ROPE_THETA = 10000.0
MEM_HEADS = 4
MEM_HEAD_DIM = 512
D_FF = 8192
LANES = 128
MIB = 1024 * 1024


def _params(semantics, vmem_mib):
    return pltpu.CompilerParams(dimension_semantics=semantics, vmem_limit_bytes=vmem_mib * MIB)


def _rms(x, g):
    ms = jnp.mean(x * x, axis=-1, keepdims=True)
    return x * lax.rsqrt(ms + RMS_EPS) * g


def _norm_rows_into(x_ref, g_ref, xn_ref, rows=256):
    def body(r, c):
        sl = pl.ds(pl.multiple_of(r * rows, rows), rows)
        xn_ref[sl, :] = _rms(x_ref[sl, :], g_ref[...]).astype(BF16)
        return c
    lax.fori_loop(0, x_ref.shape[0] // rows, body, 0)


NORM_CHUNK_ROWS = 256


def _norm_mm_kernel(x_ref, g_ref, w_ref, o_ref, *scratch, n_col_steps):
    tm = x_ref.shape[0]

    def first():
        for r in range(tm // NORM_CHUNK_ROWS):
            sl = slice(r * NORM_CHUNK_ROWS, (r + 1) * NORM_CHUNK_ROWS)
            xn = _rms(x_ref[sl, :], g_ref[...]).astype(BF16)
            if n_col_steps > 1:
                scratch[0][sl, :] = xn
            o_ref[sl, :] = jnp.dot(xn, w_ref[...], preferred_element_type=F32).astype(o_ref.dtype)

    if n_col_steps == 1:
        first()
    else:
        pl.when(pl.program_id(1) == 0)(first)

        @pl.when(pl.program_id(1) != 0)
        def _():
            o_ref[...] = jnp.dot(scratch[0][...], w_ref[...], preferred_element_type=F32).astype(o_ref.dtype)


def norm_matmul(x, g, w, widx, *, tm, tn, out_dtype, vmem_mib, name):
    m, k = x.shape
    n = w.shape[2]
    n_col_steps = n // tn
    w_mode = dict(pipeline_mode=pl.Buffered(1)) if n_col_steps == 1 else {}
    return pl.pallas_call(
        functools.partial(_norm_mm_kernel, n_col_steps=n_col_steps),
        grid=(m // tm, n_col_steps),
        in_specs=[pl.BlockSpec((tm, k), lambda i, j: (i, 0)),
                  pl.BlockSpec((1, k), lambda i, j: (0, 0)),
                  pl.BlockSpec((None, k, tn), lambda i, j: (widx, 0, j), **w_mode)],
        out_specs=pl.BlockSpec((tm, tn), lambda i, j: (i, j)),
        out_shape=jax.ShapeDtypeStruct((m, n), out_dtype),
        scratch_shapes=[pltpu.VMEM((tm, k), BF16)] if n_col_steps > 1 else [],
        compiler_params=_params(("parallel", "arbitrary"), vmem_mib),
        name=name,
    )(x, g.reshape(1, k), w)


def _mm_res_kernel(a_ref, w_ref, r_ref, o_ref):
    o_ref[...] = r_ref[...] + jnp.dot(a_ref[...], w_ref[...], preferred_element_type=F32)


def matmul_residual(a, w, widx, res, *, tm, name):
    m, k = a.shape
    n = w.shape[2]
    return pl.pallas_call(
        _mm_res_kernel,
        grid=(m // tm,),
        in_specs=[pl.BlockSpec((tm, k), lambda i: (i, 0)),
                  pl.BlockSpec((None, k, n), lambda i: (widx, 0, 0), pipeline_mode=pl.Buffered(1)),
                  pl.BlockSpec((tm, n), lambda i: (i, 0))],
        out_specs=pl.BlockSpec((tm, n), lambda i: (i, 0)),
        out_shape=jax.ShapeDtypeStruct((m, n), F32),
        compiler_params=_params(("parallel",), 40),
        name=name,
    )(a, w, res)


def _ffn_kernel(x_ref, g_ref, w1_ref, w2_ref, gf_ref, o_ref, xn_ref, *, final_norm):
    f = pl.program_id(1)

    @pl.when(f == 0)
    def _():
        _norm_rows_into(x_ref, g_ref, xn_ref)
        o_ref[...] = x_ref[...]

    h = jnp.dot(xn_ref[...], w1_ref[...], preferred_element_type=F32)
    h = jnp.maximum(h, 0.0)
    h = (h * h).astype(BF16)
    o_ref[...] += jnp.dot(h, w2_ref[...], preferred_element_type=F32)

    if final_norm:
        @pl.when(f == pl.num_programs(1) - 1)
        def _():
            rows = 256

            def body(r, c):
                sl = pl.ds(pl.multiple_of(r * rows, rows), rows)
                o_ref[sl, :] = _rms(o_ref[sl, :], gf_ref[...])
                return c
            lax.fori_loop(0, o_ref.shape[0] // rows, body, 0)


def ffn(x, g, w1, w2, widx, gf, *, tm, tf, final_norm, name):
    m, d = x.shape
    dff = w1.shape[2]
    return pl.pallas_call(
        functools.partial(_ffn_kernel, final_norm=final_norm),
        grid=(m // tm, dff // tf),
        in_specs=[pl.BlockSpec((tm, d), lambda i, f: (i, 0)),
                  pl.BlockSpec((1, d), lambda i, f: (0, 0)),
                  pl.BlockSpec((None, d, tf), lambda i, f: (widx, 0, f)),
                  pl.BlockSpec((None, tf, d), lambda i, f: (widx, f, 0)),
                  pl.BlockSpec((1, d), lambda i, f: (0, 0))],
        out_specs=pl.BlockSpec((tm, d), lambda i, f: (i, 0)),
        out_shape=jax.ShapeDtypeStruct((m, d), F32),
        scratch_shapes=[pltpu.VMEM((tm, d), BF16)],
        compiler_params=_params(("parallel", "arbitrary"), 52),
        name=name,
    )(x, g.reshape(1, d), w1, w2, gf.reshape(1, d))


def _gelu(x):
    c = 0.7978845608028654
    return 0.5 * x * (1.0 + jnp.tanh(c * (x + 0.044715 * (x * x * x))))


EVEN_CHUNK_ROWS = 256
CONV_TAIL_ROWS = 8


def _even_mixer_kernel(x_ref, g_ref, win_ref, vg_ref, ws_ref, bs_ref, cw_ref, wout_ref, o_ref, tail_ref, *, tm):
    @pl.when((pl.program_id(0) * tm) % SEQ == 0)
    def _():
        tail_ref[...] = jnp.zeros_like(tail_ref)

    ri = lax.broadcasted_iota(jnp.int32, (GMLP_BLOCK, GMLP_BLOCK), 0) // CHUNK
    ci = lax.broadcasted_iota(jnp.int32, (GMLP_BLOCK, GMLP_BLOCK), 1) // CHUNK
    keep = ci <= ri
    w_mix = [jnp.where(keep, ws_ref[g], 0.0).astype(BF16) for g in range(A_GROUPS)]
    w0 = cw_ref[0:1, :]
    w1 = cw_ref[1:2, :]
    w2 = cw_ref[2:3, :]

    tail = tail_ref[...]
    for r in range(tm // EVEN_CHUNK_ROWS):
        sl = slice(r * EVEN_CHUNK_ROWS, (r + 1) * EVEN_CHUNK_ROWS)
        x = x_ref[sl, :]
        xn = _rms(x, g_ref[...]).astype(BF16)
        z = jnp.dot(xn, win_ref[...], preferred_element_type=F32)

        cols = []
        for g in range(A_GROUPS):
            c0 = g * LANES
            u = _gelu(z[:, c0:c0 + LANES])
            v = _gelu(z[:, A_WIDTH + c0:A_WIDTH + c0 + LANES])
            vn = _rms(v, vg_ref[:, c0:c0 + LANES]).astype(BF16)
            mixed = jnp.concatenate(
                [jnp.dot(w_mix[g], vn[b0:b0 + GMLP_BLOCK], preferred_element_type=F32) + bs_ref[:, g:g + 1]
                 for b0 in range(0, EVEN_CHUNK_ROWS, GMLP_BLOCK)], axis=0)
            cols.append((u * mixed).astype(BF16))

        zc = z[:, 3 * 1024:4 * 1024] * z[:, 4 * 1024:5 * 1024]
        big = jnp.concatenate([tail, zc], axis=0)
        s1 = pltpu.roll(big, 1, axis=0)[CONV_TAIL_ROWS:]
        s2 = pltpu.roll(big, 2, axis=0)[CONV_TAIL_ROWS:]
        conv = w2 * zc + w1 * s1 + w0 * s2
        cols.append((z[:, 2 * 1024:3 * 1024] * conv).astype(BF16))
        tail = zc[EVEN_CHUNK_ROWS - CONV_TAIL_ROWS:]

        y = jnp.concatenate(cols, axis=1)
        o_ref[sl, :] = x + jnp.dot(y, wout_ref[...], preferred_element_type=F32)
    tail_ref[...] = tail


def even_mixer(x, g, w_in, v_norm_g, w_s, b_s, conv_w, w_out, widx, *, tm, name):
    t, d = x.shape
    full = lambda shape: pl.BlockSpec(shape, lambda i: (0,) * len(shape))
    resident = lambda w: pl.BlockSpec((None,) + w.shape[1:], lambda i: (widx, 0, 0), pipeline_mode=pl.Buffered(1))
    return pl.pallas_call(
        functools.partial(_even_mixer_kernel, tm=tm),
        grid=(t // tm,),
        in_specs=[pl.BlockSpec((tm, d), lambda i: (i, 0)),
                  full((1, d)), resident(w_in), full((1, A_WIDTH)),
                  full((A_GROUPS, GMLP_BLOCK, GMLP_BLOCK)), full((GMLP_BLOCK, A_GROUPS)), full((3, B_WIDTH)),
                  resident(w_out)],
        out_specs=pl.BlockSpec((tm, d), lambda i: (i, 0)),
        out_shape=jax.ShapeDtypeStruct((t, d), F32),
        scratch_shapes=[pltpu.VMEM((CONV_TAIL_ROWS, B_WIDTH), F32)],
        compiler_params=_params(("arbitrary",), 56),
        name=name,
    )(x, g.reshape(1, d), w_in, v_norm_g.reshape(1, A_WIDTH), w_s, b_s.T, conv_w, w_out)


def _rope_table_kernel(pos_ref, inv_ref, sign_ref, cs_ref, sn_ref):
    ang = pos_ref[...].astype(F32) * inv_ref[...]
    cs_ref[...] = jnp.cos(ang)
    sn_ref[...] = jnp.sin(ang) * sign_ref[...]


def rope_tables(positions, *, tm=2048):
    t = positions.size
    half = C_ROPE // 2
    inv = ROPE_THETA ** (-jnp.arange(0, C_ROPE, 2, dtype=F32) / C_ROPE)
    inv128 = jnp.tile(inv, LANES // half).reshape(1, LANES)
    sign = jnp.tile(jnp.concatenate([-jnp.ones((half,), F32), jnp.ones((half,), F32)]), 2).reshape(1, LANES)
    return pl.pallas_call(
        _rope_table_kernel,
        grid=(t // tm,),
        in_specs=[pl.BlockSpec((tm, 1), lambda i: (i, 0)),
                  pl.BlockSpec((1, LANES), lambda i: (0, 0)),
                  pl.BlockSpec((1, LANES), lambda i: (0, 0))],
        out_specs=[pl.BlockSpec((tm, LANES), lambda i: (i, 0)),
                   pl.BlockSpec((tm, LANES), lambda i: (i, 0))],
        out_shape=[jax.ShapeDtypeStruct((t, LANES), F32)] * 2,
        compiler_params=_params(("parallel",), 32),
        name="rope_tables",
    )(positions.reshape(t, 1), inv128, sign)


def _swap_halves(x):
    lane = lax.broadcasted_iota(jnp.int32, x.shape, 1)
    first = (lane % C_ROPE) < (C_ROPE // 2)
    return jnp.where(first, pltpu.roll(x, LANES - C_ROPE // 2, axis=1), pltpu.roll(x, C_ROPE // 2, axis=1))


def _mla_prep_kernel(x_ref, g_ref, win_ref, qg_ref, kvg_ref, wuq_ref, wukv_ref, cs_ref, sn_ref,
                     qn_ref, qr_ref, kn_ref, v_ref, kr_ref):
    xn = _rms(x_ref[...], g_ref[...]).astype(BF16)
    z = jnp.dot(xn, win_ref[...], preferred_element_type=F32)
    cq = _rms(z[:, :C_Q_RANK], qg_ref[...]).astype(BF16)
    ckv = _rms(z[:, C_Q_RANK:C_Q_RANK + C_KV_RANK], kvg_ref[...]).astype(BF16)
    cs = cs_ref[...]
    sn = sn_ref[...]

    kr2 = z[:, C_Q_RANK + C_KV_RANK:]
    kr2 = kr2 * cs + _swap_halves(kr2) * sn
    lane = lax.broadcasted_iota(jnp.int32, kr2.shape, 1)
    kr_ref[0] = jnp.where(lane < C_ROPE, kr2, 0.0).astype(BF16)
    kr_ref[1] = jnp.where(lane >= C_ROPE, kr2, 0.0).astype(BF16)

    q = jnp.dot(cq, wuq_ref[...], preferred_element_type=F32)
    for h in range(C_HEADS):
        qn_ref[h] = q[:, h * C_NOPE:(h + 1) * C_NOPE].astype(BF16)
    for p in range(C_HEADS // 2):
        base = C_HEADS * C_NOPE + p * LANES
        xr = q[:, base:base + LANES]
        qr_ref[p] = (xr * cs + _swap_halves(xr) * sn).astype(BF16)

    kv = jnp.dot(ckv, wukv_ref[...], preferred_element_type=F32)
    for h in range(C_HEADS):
        kn_ref[h] = kv[:, h * C_NOPE:(h + 1) * C_NOPE].astype(BF16)
        v_ref[h] = kv[:, C_HEADS * C_NOPE + h * C_V:C_HEADS * C_NOPE + (h + 1) * C_V].astype(BF16)


def mla_prep(x, g, w_in, q_g, kv_g, w_uq, w_ukv, widx, cs, sn, *, tm, name):
    t, d = x.shape
    full = lambda shape: pl.BlockSpec(shape, lambda i: (0,) * len(shape))
    layer = lambda w: pl.BlockSpec((None,) + w.shape[1:], lambda i: (widx, 0, 0))
    hspec = lambda n: pl.BlockSpec((n, tm, LANES), lambda i: (0, i, 0))
    hshape = lambda n: jax.ShapeDtypeStruct((n, t, LANES), BF16)
    return pl.pallas_call(
        _mla_prep_kernel,
        grid=(t // tm,),
        in_specs=[pl.BlockSpec((tm, d), lambda i: (i, 0)),
                  full((1, d)), layer(w_in), full((1, C_Q_RANK)), full((1, C_KV_RANK)),
                  layer(w_uq), layer(w_ukv),
                  pl.BlockSpec((tm, LANES), lambda i: (i, 0)),
                  pl.BlockSpec((tm, LANES), lambda i: (i, 0))],
        out_specs=[hspec(C_HEADS), hspec(C_HEADS // 2), hspec(C_HEADS), hspec(C_HEADS), hspec(2)],
        out_shape=[hshape(C_HEADS), hshape(C_HEADS // 2), hshape(C_HEADS), hshape(C_HEADS), hshape(2)],
        compiler_params=_params(("parallel",), 48),
        name=name,
    )(x, g.reshape(1, d), w_in, q_g.reshape(1, -1), kv_g.reshape(1, -1), w_uq, w_ukv, cs, sn)


def _mla_attn_kernel(qn_ref, qr_ref, kn_ref, v_ref, kr_ref, o_ref, *, tq):
    exp_mult = (C_NOPE + C_ROPE) ** -0.5 * 1.4426950408889634
    nq = SEQ // tq
    rq = lax.broadcasted_iota(jnp.int32, (tq, tq), 0) // CHUNK
    ck = lax.broadcasted_iota(jnp.int32, (tq, tq), 1) // CHUNK
    diag_keep = ck <= rq
    nt = (((1,), (1,)), ((), ()))

    def keys(hh, lo, hi):
        return jnp.concatenate([kn_ref[hh, lo:hi, :], kr_ref[hh, lo:hi, :]], axis=1)

    for hh in range(2):
        for t in range(nq):
            lo, hi = t * tq, (t + 1) * tq
            q = jnp.concatenate([qn_ref[hh, lo:hi, :], qr_ref[0, lo:hi, :]], axis=1)
            s_d = lax.dot_general(q, keys(hh, lo, hi), nt, preferred_element_type=F32)
            s_d = jnp.where(diag_keep, s_d, -jnp.inf)
            m = jnp.max(s_d, axis=-1, keepdims=True)
            if t > 0:
                s_p = lax.dot_general(q, keys(hh, 0, lo), nt, preferred_element_type=F32)
                m = jnp.maximum(m, jnp.max(s_p, axis=-1, keepdims=True))
            p_d = jnp.exp2((s_d - m) * exp_mult)
            l = jnp.sum(p_d, axis=-1, keepdims=True)
            acc = jnp.dot(p_d.astype(BF16), v_ref[hh, lo:hi, :], preferred_element_type=F32)
            if t > 0:
                p_p = jnp.exp2((s_p - m) * exp_mult)
                l = l + jnp.sum(p_p, axis=-1, keepdims=True)
                acc = acc + jnp.dot(p_p.astype(BF16), v_ref[hh, 0:lo, :], preferred_element_type=F32)
            o_ref[lo:hi, hh * C_V:(hh + 1) * C_V] = (acc / l).astype(o_ref.dtype)


def mla_attention(qn, qr, kn, v, kr, *, bsz, tq, name):
    t = qn.shape[1]
    return pl.pallas_call(
        functools.partial(_mla_attn_kernel, tq=tq),
        grid=(bsz, C_HEADS // 2),
        in_specs=[pl.BlockSpec((2, SEQ, LANES), lambda b, p: (p, b, 0)),
                  pl.BlockSpec((1, SEQ, LANES), lambda b, p: (p, b, 0)),
                  pl.BlockSpec((2, SEQ, LANES), lambda b, p: (p, b, 0)),
                  pl.BlockSpec((2, SEQ, LANES), lambda b, p: (p, b, 0)),
                  pl.BlockSpec((2, SEQ, LANES), lambda b, p: (0, b, 0))],
        out_specs=pl.BlockSpec((SEQ, 2 * C_V), lambda b, p: (b, p)),
        out_shape=jax.ShapeDtypeStruct((t, C_HEADS * C_V), BF16),
        compiler_params=_params(("parallel", "parallel"), 32),
        name=name,
    )(qn, qr, kn, v, kr)


MEM_CHUNK_ROWS = 256


def _mem_block_kernel(x_ref, g_ref, wq_ref, k_ref, v_ref, wo_ref, o_ref):
    exp_mult = MEM_HEAD_DIM ** -0.5 * 1.4426950408889634
    nt = (((1,), (1,)), ((), ()))
    for r in range(x_ref.shape[0] // MEM_CHUNK_ROWS):
        sl = slice(r * MEM_CHUNK_ROWS, (r + 1) * MEM_CHUNK_ROWS)
        x = x_ref[sl, :]
        xn = _rms(x, g_ref[...]).astype(BF16)
        q = jnp.dot(xn, wq_ref[...], preferred_element_type=F32).astype(BF16)
        heads = []
        for h in range(MEM_HEADS):
            c0 = h * MEM_HEAD_DIM
            s = lax.dot_general(q[:, c0:c0 + MEM_HEAD_DIM], k_ref[:, c0:c0 + MEM_HEAD_DIM], nt,
                                preferred_element_type=F32)
            m = jnp.max(s, axis=-1, keepdims=True)
            p = jnp.exp2((s - m) * exp_mult)
            l = jnp.sum(p, axis=-1, keepdims=True)
            o = jnp.dot(p.astype(BF16), v_ref[:, c0:c0 + MEM_HEAD_DIM], preferred_element_type=F32)
            heads.append((o / l).astype(BF16))
        att = jnp.concatenate(heads, axis=1)
        o_ref[sl, :] = x + jnp.dot(att, wo_ref[...], preferred_element_type=F32)


def mem_block(x, g, wq, k, v, wo, widx, *, tm, name):
    t, d = x.shape
    per_b = SEQ // tm
    resident = lambda: pl.BlockSpec((None, d, d), lambda i: (widx, 0, 0), pipeline_mode=pl.Buffered(1))
    return pl.pallas_call(
        _mem_block_kernel,
        grid=(t // tm,),
        in_specs=[pl.BlockSpec((tm, d), lambda i: (i, 0)),
                  pl.BlockSpec((1, d), lambda i: (0, 0)),
                  resident(),
                  pl.BlockSpec((N_MEM, d), lambda i: (i // per_b, 0)),
                  pl.BlockSpec((N_MEM, d), lambda i: (i // per_b, 0)),
                  resident()],
        out_specs=pl.BlockSpec((tm, d), lambda i: (i, 0)),
        out_shape=jax.ShapeDtypeStruct((t, d), F32),
        compiler_params=_params(("parallel",), 48),
        name=name,
    )(x, g.reshape(1, d), wq, k, v, wo)


def kernel(x, mem, positions, norm_mix_g, norm_mem_q_g, norm_mem_kv_g, norm_ffn_g, final_norm_g, ab_w_in, a_v_norm_g, a_w_s, a_b_s, b_conv_w, ab_w_out, c_w_in, c_q_norm_g, c_kv_norm_g, c_w_uq, c_w_ukv, c_w_out, m_wq, m_wk, m_wv, m_wo, f_w1, f_w2):
    bsz, seq, d = x.shape
    depth = norm_mix_g.shape[0]
    t = bsz * seq
    xf = x.reshape(t, d)
    memf = mem.reshape(bsz * N_MEM, d)
    bf = lambda a: a.astype(BF16)

    w_ab_in, w_ab_out, w_c_out = bf(ab_w_in), bf(ab_w_out), bf(c_w_out)
    w_mq, w_mk, w_mv, w_mo = bf(m_wq), bf(m_wk), bf(m_wv), bf(m_wo)
    w_f1, w_f2 = bf(f_w1), bf(f_w2)
    n_odd = c_w_in.shape[0]
    w_c_in = bf(jnp.concatenate([c_w_in, c_w_in[:, :, C_Q_RANK + C_KV_RANK:]], axis=2))
    wuq = c_w_uq.reshape(n_odd, C_Q_RANK, C_HEADS, C_NOPE + C_ROPE)
    w_uq = bf(jnp.concatenate([wuq[..., :C_NOPE].reshape(n_odd, C_Q_RANK, -1),
                               wuq[..., C_NOPE:].reshape(n_odd, C_Q_RANK, -1)], axis=2))
    wukv = c_w_ukv.reshape(n_odd, C_KV_RANK, C_HEADS, C_NOPE + C_V)
    w_ukv = bf(jnp.concatenate([wukv[..., :C_NOPE].reshape(n_odd, C_KV_RANK, -1),
                                wukv[..., C_NOPE:].reshape(n_odd, C_KV_RANK, -1)], axis=2))

    cs, sn = rope_tables(positions)

    for layer in range(depth):
        if layer % 2 == 0:
            e = layer // 2
            xf = even_mixer(xf, norm_mix_g[layer], w_ab_in, a_v_norm_g[e], a_w_s[e], a_b_s[e], b_conv_w[e],
                            w_ab_out, e, tm=512, name=f"even_mixer_{layer}")
        else:
            o = layer // 2
            qn, qr, kn, v, kr = mla_prep(xf, norm_mix_g[layer], w_c_in, c_q_norm_g[o], c_kv_norm_g[o],
                                         w_uq, w_ukv, o, cs, sn, tm=512, name=f"mla_prep_{layer}")
            att = mla_attention(qn, qr, kn, v, kr, bsz=bsz, tq=256, name=f"mla_attn_{layer}")
            xf = matmul_residual(att, w_c_out, o, xf, tm=512, name=f"c_out_{layer}")

        mk = norm_matmul(memf, norm_mem_kv_g[layer], w_mk, layer, tm=512, tn=d, out_dtype=BF16, vmem_mib=32,
                         name=f"mem_k_{layer}")
        mv = norm_matmul(memf, norm_mem_kv_g[layer], w_mv, layer, tm=512, tn=d, out_dtype=BF16, vmem_mib=32,
                         name=f"mem_v_{layer}")
        xf = mem_block(xf, norm_mem_q_g[layer], w_mq, mk, mv, w_mo, layer, tm=512, name=f"mem_block_{layer}")

        xf = ffn(xf, norm_ffn_g[layer], w_f1, w_f2, layer, final_norm_g, tm=512, tf=1024,
                 final_norm=(layer == depth - 1), name=f"ffn_{layer}")

    return xf.reshape(bsz, seq, d)
```

```python
import functools

import jax
import jax.numpy as jnp
from jax import lax
from jax.experimental import pallas as pl
from jax.experimental.pallas import tpu as pltpu

F32 = jnp.float32
BF16 = jnp.bfloat16

D_MODEL = 2048
SEQ = 2048
CHUNK = 64
N_MEM = 256
RMS_EPS = 1e-6
GMLP_BLOCK = 128
A_WIDTH = 1024
A_GROUPS = 8
B_WIDTH = 1024
C_HEADS = 16
C_NOPE = 128
C_ROPE = 64
C_V = 128
C_Q_RANK = 512
C_KV_RANK = 256
ROPE_THETA = 10000.0
MEM_HEADS = 4
MEM_HEAD_DIM = 512
D_FF = 8192
LANES = 128
MIB = 1024 * 1024


def _params(semantics, vmem_mib):
    return pltpu.CompilerParams(dimension_semantics=semantics, vmem_limit_bytes=vmem_mib * MIB)


def _rms(x, g):
    ms = jnp.mean(x * x, axis=-1, keepdims=True)
    return x * lax.rsqrt(ms + RMS_EPS) * g


def _norm_rows_into(x_ref, g_ref, xn_ref, rows=256):
    def body(r, c):
        sl = pl.ds(pl.multiple_of(r * rows, rows), rows)
        xn_ref[sl, :] = _rms(x_ref[sl, :], g_ref[...]).astype(BF16)
        return c
    lax.fori_loop(0, x_ref.shape[0] // rows, body, 0)


NORM_CHUNK_ROWS = 256


def _norm_mm_kernel(x_ref, g_ref, w_ref, o_ref, *scratch, n_col_steps):
    tm = x_ref.shape[0]

    def first():
        for r in range(tm // NORM_CHUNK_ROWS):
            sl = slice(r * NORM_CHUNK_ROWS, (r + 1) * NORM_CHUNK_ROWS)
            xn = _rms(x_ref[sl, :], g_ref[...]).astype(BF16)
            if n_col_steps > 1:
                scratch[0][sl, :] = xn
            o_ref[sl, :] = jnp.dot(xn, w_ref[...], preferred_element_type=F32).astype(o_ref.dtype)

    if n_col_steps == 1:
        first()
    else:
        pl.when(pl.program_id(1) == 0)(first)

        @pl.when(pl.program_id(1) != 0)
        def _():
            o_ref[...] = jnp.dot(scratch[0][...], w_ref[...], preferred_element_type=F32).astype(o_ref.dtype)


def norm_matmul(x, g, w, widx, *, tm, tn, out_dtype, vmem_mib, name):
    m, k = x.shape
    n = w.shape[2]
    n_col_steps = n // tn
    w_mode = dict(pipeline_mode=pl.Buffered(1)) if n_col_steps == 1 else {}
    return pl.pallas_call(
        functools.partial(_norm_mm_kernel, n_col_steps=n_col_steps),
        grid=(m // tm, n_col_steps),
        in_specs=[pl.BlockSpec((tm, k), lambda i, j: (i, 0)),
                  pl.BlockSpec((1, k), lambda i, j: (0, 0)),
                  pl.BlockSpec((None, k, tn), lambda i, j: (widx, 0, j), **w_mode)],
        out_specs=pl.BlockSpec((tm, tn), lambda i, j: (i, j)),
        out_shape=jax.ShapeDtypeStruct((m, n), out_dtype),
        scratch_shapes=[pltpu.VMEM((tm, k), BF16)] if n_col_steps > 1 else [],
        compiler_params=_params(("parallel", "arbitrary"), vmem_mib),
        name=name,
    )(x, g.reshape(1, k), w)


def _mm_res_kernel(a_ref, w_ref, r_ref, o_ref):
    o_ref[...] = r_ref[...] + jnp.dot(a_ref[...], w_ref[...], preferred_element_type=F32)


def matmul_residual(a, w, widx, res, *, tm, name):
    m, k = a.shape
    n = w.shape[2]
    return pl.pallas_call(
        _mm_res_kernel,
        grid=(m // tm,),
        in_specs=[pl.BlockSpec((tm, k), lambda i: (i, 0)),
                  pl.BlockSpec((None, k, n), lambda i: (widx, 0, 0), pipeline_mode=pl.Buffered(1)),
                  pl.BlockSpec((tm, n), lambda i: (i, 0))],
        out_specs=pl.BlockSpec((tm, n), lambda i: (i, 0)),
        out_shape=jax.ShapeDtypeStruct((m, n), F32),
        compiler_params=_params(("parallel",), 40),
        name=name,
    )(a, w, res)


def _ffn_kernel(x_ref, g_ref, w1_ref, w2_ref, gf_ref, o_ref, xn_ref, *, final_norm):
    f = pl.program_id(1)

    @pl.when(f == 0)
    def _():
        _norm_rows_into(x_ref, g_ref, xn_ref)
        o_ref[...] = x_ref[...]

    h = jnp.dot(xn_ref[...], w1_ref[...], preferred_element_type=F32)
    h = jnp.maximum(h, 0.0)
    h = (h * h).astype(BF16)
    o_ref[...] += jnp.dot(h, w2_ref[...], preferred_element_type=F32)

    if final_norm:
        @pl.when(f == pl.num_programs(1) - 1)
        def _():
            rows = 256

            def body(r, c):
                sl = pl.ds(pl.multiple_of(r * rows, rows), rows)
                o_ref[sl, :] = _rms(o_ref[sl, :], gf_ref[...])
                return c
            lax.fori_loop(0, o_ref.shape[0] // rows, body, 0)


def ffn(x, g, w1, w2, widx, gf, *, tm, tf, final_norm, name):
    m, d = x.shape
    dff = w1.shape[2]
    return pl.pallas_call(
        functools.partial(_ffn_kernel, final_norm=final_norm),
        grid=(m // tm, dff // tf),
        in_specs=[pl.BlockSpec((tm, d), lambda i, f: (i, 0)),
                  pl.BlockSpec((1, d), lambda i, f: (0, 0)),
                  pl.BlockSpec((None, d, tf), lambda i, f: (widx, 0, f)),
                  pl.BlockSpec((None, tf, d), lambda i, f: (widx, f, 0)),
                  pl.BlockSpec((1, d), lambda i, f: (0, 0))],
        out_specs=pl.BlockSpec((tm, d), lambda i, f: (i, 0)),
        out_shape=jax.ShapeDtypeStruct((m, d), F32),
        scratch_shapes=[pltpu.VMEM((tm, d), BF16)],
        compiler_params=_params(("parallel", "arbitrary"), 52),
        name=name,
    )(x, g.reshape(1, d), w1, w2, gf.reshape(1, d))


def _gelu(x):
    c = 0.7978845608028654
    return 0.5 * x * (1.0 + jnp.tanh(c * (x + 0.044715 * (x * x * x))))


EVEN_CHUNK_ROWS = 256
CONV_TAIL_ROWS = 8


def _even_mixer_kernel(x_ref, g_ref, win_ref, vg_ref, ws_ref, bs_ref, cw_ref, wout_ref, o_ref, tail_ref, *, tm):
    @pl.when((pl.program_id(0) * tm) % SEQ == 0)
    def _():
        tail_ref[...] = jnp.zeros_like(tail_ref)

    ri = lax.broadcasted_iota(jnp.int32, (GMLP_BLOCK, GMLP_BLOCK), 0) // CHUNK
    ci = lax.broadcasted_iota(jnp.int32, (GMLP_BLOCK, GMLP_BLOCK), 1) // CHUNK
    keep = ci <= ri
    w_mix = [jnp.where(keep, ws_ref[g], 0.0).astype(BF16) for g in range(A_GROUPS)]
    w0 = cw_ref[0:1, :]
    w1 = cw_ref[1:2, :]
    w2 = cw_ref[2:3, :]

    tail = tail_ref[...]
    for r in range(tm // EVEN_CHUNK_ROWS):
        sl = slice(r * EVEN_CHUNK_ROWS, (r + 1) * EVEN_CHUNK_ROWS)
        x = x_ref[sl, :]
        xn = _rms(x, g_ref[...]).astype(BF16)
        z = jnp.dot(xn, win_ref[...], preferred_element_type=F32)

        cols = []
        for g in range(A_GROUPS):
            c0 = g * LANES
            u = _gelu(z[:, c0:c0 + LANES])
            v = _gelu(z[:, A_WIDTH + c0:A_WIDTH + c0 + LANES])
            vn = _rms(v, vg_ref[:, c0:c0 + LANES]).astype(BF16)
            mixed = jnp.concatenate(
                [jnp.dot(w_mix[g], vn[b0:b0 + GMLP_BLOCK], preferred_element_type=F32) + bs_ref[:, g:g + 1]
                 for b0 in range(0, EVEN_CHUNK_ROWS, GMLP_BLOCK)], axis=0)
            cols.append((u * mixed).astype(BF16))

        zc = z[:, 3 * 1024:4 * 1024] * z[:, 4 * 1024:5 * 1024]
        big = jnp.concatenate([tail, zc], axis=0)
        s1 = pltpu.roll(big, 1, axis=0)[CONV_TAIL_ROWS:]
        s2 = pltpu.roll(big, 2, axis=0)[CONV_TAIL_ROWS:]
        conv = w2 * zc + w1 * s1 + w0 * s2
        cols.append((z[:, 2 * 1024:3 * 1024] * conv).astype(BF16))
        tail = zc[EVEN_CHUNK_ROWS - CONV_TAIL_ROWS:]

        y = jnp.concatenate(cols, axis=1)
        o_ref[sl, :] = x + jnp.dot(y, wout_ref[...], preferred_element_type=F32)
    tail_ref[...] = tail


def even_mixer(x, g, w_in, v_norm_g, w_s, b_s, conv_w, w_out, widx, *, tm, name):
    t, d = x.shape
    full = lambda shape: pl.BlockSpec(shape, lambda i: (0,) * len(shape))
    resident = lambda w: pl.BlockSpec((None,) + w.shape[1:], lambda i: (widx, 0, 0), pipeline_mode=pl.Buffered(1))
    return pl.pallas_call(
        functools.partial(_even_mixer_kernel, tm=tm),
        grid=(t // tm,),
        in_specs=[pl.BlockSpec((tm, d), lambda i: (i, 0)),
                  full((1, d)), resident(w_in), full((1, A_WIDTH)),
                  full((A_GROUPS, GMLP_BLOCK, GMLP_BLOCK)), full((GMLP_BLOCK, A_GROUPS)), full((3, B_WIDTH)),
                  resident(w_out)],
        out_specs=pl.BlockSpec((tm, d), lambda i: (i, 0)),
        out_shape=jax.ShapeDtypeStruct((t, d), F32),
        scratch_shapes=[pltpu.VMEM((CONV_TAIL_ROWS, B_WIDTH), F32)],
        compiler_params=_params(("arbitrary",), 56),
        name=name,
    )(x, g.reshape(1, d), w_in, v_norm_g.reshape(1, A_WIDTH), w_s, b_s.T, conv_w, w_out)


def _rope_table_kernel(pos_ref, inv_ref, sign_ref, cs_ref, sn_ref):
    ang = pos_ref[...].astype(F32) * inv_ref[...]
    cs_ref[...] = jnp.cos(ang)
    sn_ref[...] = jnp.sin(ang) * sign_ref[...]


def rope_tables(positions, *, tm=2048):
    t = positions.size
    half = C_ROPE // 2
    inv = ROPE_THETA ** (-jnp.arange(0, C_ROPE, 2, dtype=F32) / C_ROPE)
    inv128 = jnp.tile(inv, LANES // half).reshape(1, LANES)
    sign = jnp.tile(jnp.concatenate([-jnp.ones((half,), F32), jnp.ones((half,), F32)]), 2).reshape(1, LANES)
    return pl.pallas_call(
        _rope_table_kernel,
        grid=(t // tm,),
        in_specs=[pl.BlockSpec((tm, 1), lambda i: (i, 0)),
                  pl.BlockSpec((1, LANES), lambda i: (0, 0)),
                  pl.BlockSpec((1, LANES), lambda i: (0, 0))],
        out_specs=[pl.BlockSpec((tm, LANES), lambda i: (i, 0)),
                   pl.BlockSpec((tm, LANES), lambda i: (i, 0))],
        out_shape=[jax.ShapeDtypeStruct((t, LANES), F32)] * 2,
        compiler_params=_params(("parallel",), 32),
        name="rope_tables",
    )(positions.reshape(t, 1), inv128, sign)


def _swap_halves(x):
    lane = lax.broadcasted_iota(jnp.int32, x.shape, 1)
    first = (lane % C_ROPE) < (C_ROPE // 2)
    return jnp.where(first, pltpu.roll(x, LANES - C_ROPE // 2, axis=1), pltpu.roll(x, C_ROPE // 2, axis=1))


def _mla_prep_kernel(x_ref, g_ref, win_ref, qg_ref, kvg_ref, wuq_ref, wukv_ref, cs_ref, sn_ref,
                     qn_ref, qr_ref, kn_ref, v_ref, kr_ref):
    xn = _rms(x_ref[...], g_ref[...]).astype(BF16)
    z = jnp.dot(xn, win_ref[...], preferred_element_type=F32)
    cq = _rms(z[:, :C_Q_RANK], qg_ref[...]).astype(BF16)
    ckv = _rms(z[:, C_Q_RANK:C_Q_RANK + C_KV_RANK], kvg_ref[...]).astype(BF16)
    cs = cs_ref[...]
    sn = sn_ref[...]

    kr2 = z[:, C_Q_RANK + C_KV_RANK:]
    kr2 = kr2 * cs + _swap_halves(kr2) * sn
    lane = lax.broadcasted_iota(jnp.int32, kr2.shape, 1)
    kr_ref[0] = jnp.where(lane < C_ROPE, kr2, 0.0).astype(BF16)
    kr_ref[1] = jnp.where(lane >= C_ROPE, kr2, 0.0).astype(BF16)

    q = jnp.dot(cq, wuq_ref[...], preferred_element_type=F32)
    for h in range(C_HEADS):
        qn_ref[h] = q[:, h * C_NOPE:(h + 1) * C_NOPE].astype(BF16)
    for p in range(C_HEADS // 2):
        base = C_HEADS * C_NOPE + p * LANES
        xr = q[:, base:base + LANES]
        qr_ref[p] = (xr * cs + _swap_halves(xr) * sn).astype(BF16)

    kv = jnp.dot(ckv, wukv_ref[...], preferred_element_type=F32)
    for h in range(C_HEADS):
        kn_ref[h] = kv[:, h * C_NOPE:(h + 1) * C_NOPE].astype(BF16)
        v_ref[h] = kv[:, C_HEADS * C_NOPE + h * C_V:C_HEADS * C_NOPE + (h + 1) * C_V].astype(BF16)


def mla_prep(x, g, w_in, q_g, kv_g, w_uq, w_ukv, widx, cs, sn, *, tm, name):
    t, d = x.shape
    full = lambda shape: pl.BlockSpec(shape, lambda i: (0,) * len(shape))
    layer = lambda w: pl.BlockSpec((None,) + w.shape[1:], lambda i: (widx, 0, 0))
    hspec = lambda n: pl.BlockSpec((n, tm, LANES), lambda i: (0, i, 0))
    hshape = lambda n: jax.ShapeDtypeStruct((n, t, LANES), BF16)
    return pl.pallas_call(
        _mla_prep_kernel,
        grid=(t // tm,),
        in_specs=[pl.BlockSpec((tm, d), lambda i: (i, 0)),
                  full((1, d)), layer(w_in), full((1, C_Q_RANK)), full((1, C_KV_RANK)),
                  layer(w_uq), layer(w_ukv),
                  pl.BlockSpec((tm, LANES), lambda i: (i, 0)),
                  pl.BlockSpec((tm, LANES), lambda i: (i, 0))],
        out_specs=[hspec(C_HEADS), hspec(C_HEADS // 2), hspec(C_HEADS), hspec(C_HEADS), hspec(2)],
        out_shape=[hshape(C_HEADS), hshape(C_HEADS // 2), hshape(C_HEADS), hshape(C_HEADS), hshape(2)],
        compiler_params=_params(("parallel",), 48),
        name=name,
    )(x, g.reshape(1, d), w_in, q_g.reshape(1, -1), kv_g.reshape(1, -1), w_uq, w_ukv, cs, sn)


def _mla_attn_kernel(qn_ref, qr_ref, kn_ref, v_ref, kr_ref, o_ref, *, tq):
    exp_mult = (C_NOPE + C_ROPE) ** -0.5 * 1.4426950408889634
    nq = SEQ // tq
    rq = lax.broadcasted_iota(jnp.int32, (tq, tq), 0) // CHUNK
    ck = lax.broadcasted_iota(jnp.int32, (tq, tq), 1) // CHUNK
    diag_keep = ck <= rq
    nt = (((1,), (1,)), ((), ()))

    def keys(hh, lo, hi):
        return jnp.concatenate([kn_ref[hh, lo:hi, :], kr_ref[hh, lo:hi, :]], axis=1)

    for hh in range(2):
        for t in range(nq):
            lo, hi = t * tq, (t + 1) * tq
            q = jnp.concatenate([qn_ref[hh, lo:hi, :], qr_ref[0, lo:hi, :]], axis=1)
            s_d = lax.dot_general(q, keys(hh, lo, hi), nt, preferred_element_type=F32)
            s_d = jnp.where(diag_keep, s_d, -jnp.inf)
            m = jnp.max(s_d, axis=-1, keepdims=True)
            if t > 0:
                s_p = lax.dot_general(q, keys(hh, 0, lo), nt, preferred_element_type=F32)
                m = jnp.maximum(m, jnp.max(s_p, axis=-1, keepdims=True))
            p_d = jnp.exp2((s_d - m) * exp_mult)
            l = jnp.sum(p_d, axis=-1, keepdims=True)
            acc = jnp.dot(p_d.astype(BF16), v_ref[hh, lo:hi, :], preferred_element_type=F32)
            if t > 0:
                p_p = jnp.exp2((s_p - m) * exp_mult)
                l = l + jnp.sum(p_p, axis=-1, keepdims=True)
                acc = acc + jnp.dot(p_p.astype(BF16), v_ref[hh, 0:lo, :], preferred_element_type=F32)
            o_ref[lo:hi, hh * C_V:(hh + 1) * C_V] = (acc / l).astype(o_ref.dtype)


def mla_attention(qn, qr, kn, v, kr, *, bsz, tq, name):
    t = qn.shape[1]
    return pl.pallas_call(
        functools.partial(_mla_attn_kernel, tq=tq),
        grid=(bsz, C_HEADS // 2),
        in_specs=[pl.BlockSpec((2, SEQ, LANES), lambda b, p: (p, b, 0)),
                  pl.BlockSpec((1, SEQ, LANES), lambda b, p: (p, b, 0)),
                  pl.BlockSpec((2, SEQ, LANES), lambda b, p: (p, b, 0)),
                  pl.BlockSpec((2, SEQ, LANES), lambda b, p: (p, b, 0)),
                  pl.BlockSpec((2, SEQ, LANES), lambda b, p: (0, b, 0))],
        out_specs=pl.BlockSpec((SEQ, 2 * C_V), lambda b, p: (b, p)),
        out_shape=jax.ShapeDtypeStruct((t, C_HEADS * C_V), BF16),
        compiler_params=_params(("parallel", "parallel"), 40),
        name=name,
    )(qn, qr, kn, v, kr)


MEM_CHUNK_ROWS = 256


def _mem_block_kernel(x_ref, g_ref, wq_ref, k_ref, v_ref, wo_ref, o_ref):
    exp_mult = MEM_HEAD_DIM ** -0.5 * 1.4426950408889634
    nt = (((1,), (1,)), ((), ()))
    for r in range(x_ref.shape[0] // MEM_CHUNK_ROWS):
        sl = slice(r * MEM_CHUNK_ROWS, (r + 1) * MEM_CHUNK_ROWS)
        x = x_ref[sl, :]
        xn = _rms(x, g_ref[...]).astype(BF16)
        q = jnp.dot(xn, wq_ref[...], preferred_element_type=F32).astype(BF16)
        heads = []
        for h in range(MEM_HEADS):
            c0 = h * MEM_HEAD_DIM
            s = lax.dot_general(q[:, c0:c0 + MEM_HEAD_DIM], k_ref[:, c0:c0 + MEM_HEAD_DIM], nt,
                                preferred_element_type=F32)
            m = jnp.max(s, axis=-1, keepdims=True)
            p = jnp.exp2((s - m) * exp_mult)
            l = jnp.sum(p, axis=-1, keepdims=True)
            o = jnp.dot(p.astype(BF16), v_ref[:, c0:c0 + MEM_HEAD_DIM], preferred_element_type=F32)
            heads.append((o / l).astype(BF16))
        att = jnp.concatenate(heads, axis=1)
        o_ref[sl, :] = x + jnp.dot(att, wo_ref[...], preferred_element_type=F32)


def mem_block(x, g, wq, k, v, wo, widx, *, tm, name):
    t, d = x.shape
    per_b = SEQ // tm
    resident = lambda: pl.BlockSpec((None, d, d), lambda i: (widx, 0, 0), pipeline_mode=pl.Buffered(1))
    return pl.pallas_call(
        _mem_block_kernel,
        grid=(t // tm,),
        in_specs=[pl.BlockSpec((tm, d), lambda i: (i, 0)),
                  pl.BlockSpec((1, d), lambda i: (0, 0)),
                  resident(),
                  pl.BlockSpec((N_MEM, d), lambda i: (i // per_b, 0)),
                  pl.BlockSpec((N_MEM, d), lambda i: (i // per_b, 0)),
                  resident()],
        out_specs=pl.BlockSpec((tm, d), lambda i: (i, 0)),
        out_shape=jax.ShapeDtypeStruct((t, d), F32),
        compiler_params=_params(("parallel",), 48),
        name=name,
    )(x, g.reshape(1, d), wq, k, v, wo)


def kernel(x, mem, positions, norm_mix_g, norm_mem_q_g, norm_mem_kv_g, norm_ffn_g, final_norm_g, ab_w_in, a_v_norm_g, a_w_s, a_b_s, b_conv_w, ab_w_out, c_w_in, c_q_norm_g, c_kv_norm_g, c_w_uq, c_w_ukv, c_w_out, m_wq, m_wk, m_wv, m_wo, f_w1, f_w2):
    bsz, seq, d = x.shape
    depth = norm_mix_g.shape[0]
    t = bsz * seq
    xf = x.reshape(t, d)
    memf = mem.reshape(bsz * N_MEM, d)
    bf = lambda a: a.astype(BF16)

    w_ab_in, w_ab_out, w_c_out = bf(ab_w_in), bf(ab_w_out), bf(c_w_out)
    w_mq, w_mk, w_mv, w_mo = bf(m_wq), bf(m_wk), bf(m_wv), bf(m_wo)
    w_f1, w_f2 = bf(f_w1), bf(f_w2)
    n_odd = c_w_in.shape[0]
    w_c_in = bf(jnp.concatenate([c_w_in, c_w_in[:, :, C_Q_RANK + C_KV_RANK:]], axis=2))
    wuq = c_w_uq.reshape(n_odd, C_Q_RANK, C_HEADS, C_NOPE + C_ROPE)
    w_uq = bf(jnp.concatenate([wuq[..., :C_NOPE].reshape(n_odd, C_Q_RANK, -1),
                               wuq[..., C_NOPE:].reshape(n_odd, C_Q_RANK, -1)], axis=2))
    wukv = c_w_ukv.reshape(n_odd, C_KV_RANK, C_HEADS, C_NOPE + C_V)
    w_ukv = bf(jnp.concatenate([wukv[..., :C_NOPE].reshape(n_odd, C_KV_RANK, -1),
                                wukv[..., C_NOPE:].reshape(n_odd, C_KV_RANK, -1)], axis=2))

    cs, sn = rope_tables(positions)

    for layer in range(depth):
        if layer % 2 == 0:
            e = layer // 2
            xf = even_mixer(xf, norm_mix_g[layer], w_ab_in, a_v_norm_g[e], a_w_s[e], a_b_s[e], b_conv_w[e],
                            w_ab_out, e, tm=512, name=f"even_mixer_{layer}")
        else:
            o = layer // 2
            qn, qr, kn, v, kr = mla_prep(xf, norm_mix_g[layer], w_c_in, c_q_norm_g[o], c_kv_norm_g[o],
                                         w_uq, w_ukv, o, cs, sn, tm=512, name=f"mla_prep_{layer}")
            att = mla_attention(qn, qr, kn, v, kr, bsz=bsz, tq=512, name=f"mla_attn_{layer}")
            xf = matmul_residual(att, w_c_out, o, xf, tm=512, name=f"c_out_{layer}")

        mk = norm_matmul(memf, norm_mem_kv_g[layer], w_mk, layer, tm=512, tn=d, out_dtype=BF16, vmem_mib=32,
                         name=f"mem_k_{layer}")
        mv = norm_matmul(memf, norm_mem_kv_g[layer], w_mv, layer, tm=512, tn=d, out_dtype=BF16, vmem_mib=32,
                         name=f"mem_v_{layer}")
        xf = mem_block(xf, norm_mem_q_g[layer], w_mq, mk, mv, w_mo, layer, tm=512, name=f"mem_block_{layer}")

        xf = ffn(xf, norm_ffn_g[layer], w_f1, w_f2, layer, final_norm_g, tm=512, tf=1024,
                 final_norm=(layer == depth - 1), name=f"ffn_{layer}")

    return xf.reshape(bsz, seq, d)
```

```python
import functools

import jax
import jax.numpy as jnp
from jax import lax
from jax.experimental import pallas as pl
from jax.experimental.pallas import tpu as pltpu

F32 = jnp.float32
BF16 = jnp.bfloat16

D_MODEL = 2048
SEQ = 2048
CHUNK = 64
N_MEM = 256
RMS_EPS = 1e-6
GMLP_BLOCK = 128
A_WIDTH = 1024
A_GROUPS = 8
B_WIDTH = 1024
C_HEADS = 16
C_NOPE = 128
C_ROPE = 64
C_V = 128
C_Q_RANK = 512
C_KV_RANK = 256
ROPE_THETA = 10000.0
MEM_HEADS = 4
MEM_HEAD_DIM = 512
D_FF = 8192
LANES = 128
MIB = 1024 * 1024


def _params(semantics, vmem_mib):
    return pltpu.CompilerParams(dimension_semantics=semantics, vmem_limit_bytes=vmem_mib * MIB)


def _rms(x, g):
    ms = jnp.mean(x * x, axis=-1, keepdims=True)
    return x * lax.rsqrt(ms + RMS_EPS) * g


def _norm_rows_into(x_ref, g_ref, xn_ref, rows=256):
    def body(r, c):
        sl = pl.ds(pl.multiple_of(r * rows, rows), rows)
        xn_ref[sl, :] = _rms(x_ref[sl, :], g_ref[...]).astype(BF16)
        return c
    lax.fori_loop(0, x_ref.shape[0] // rows, body, 0)


NORM_CHUNK_ROWS = 256


def _norm_mm_kernel(x_ref, g_ref, w_ref, o_ref, *scratch, n_col_steps):
    tm = x_ref.shape[0]

    def first():
        for r in range(tm // NORM_CHUNK_ROWS):
            sl = slice(r * NORM_CHUNK_ROWS, (r + 1) * NORM_CHUNK_ROWS)
            xn = _rms(x_ref[sl, :], g_ref[...]).astype(BF16)
            if n_col_steps > 1:
                scratch[0][sl, :] = xn
            o_ref[sl, :] = jnp.dot(xn, w_ref[...], preferred_element_type=F32).astype(o_ref.dtype)

    if n_col_steps == 1:
        first()
    else:
        pl.when(pl.program_id(1) == 0)(first)

        @pl.when(pl.program_id(1) != 0)
        def _():
            o_ref[...] = jnp.dot(scratch[0][...], w_ref[...], preferred_element_type=F32).astype(o_ref.dtype)


def norm_matmul(x, g, w, widx, *, tm, tn, out_dtype, vmem_mib, name):
    m, k = x.shape
    n = w.shape[2]
    n_col_steps = n // tn
    w_mode = dict(pipeline_mode=pl.Buffered(1)) if n_col_steps == 1 else {}
    return pl.pallas_call(
        functools.partial(_norm_mm_kernel, n_col_steps=n_col_steps),
        grid=(m // tm, n_col_steps),
        in_specs=[pl.BlockSpec((tm, k), lambda i, j: (i, 0)),
                  pl.BlockSpec((1, k), lambda i, j: (0, 0)),
                  pl.BlockSpec((None, k, tn), lambda i, j: (widx, 0, j), **w_mode)],
        out_specs=pl.BlockSpec((tm, tn), lambda i, j: (i, j)),
        out_shape=jax.ShapeDtypeStruct((m, n), out_dtype),
        scratch_shapes=[pltpu.VMEM((tm, k), BF16)] if n_col_steps > 1 else [],
        compiler_params=_params(("parallel", "arbitrary"), vmem_mib),
        name=name,
    )(x, g.reshape(1, k), w)


def _mm_res_kernel(a_ref, w_ref, r_ref, o_ref):
    o_ref[...] = r_ref[...] + jnp.dot(a_ref[...], w_ref[...], preferred_element_type=F32)


def matmul_residual(a, w, widx, res, *, tm, name):
    m, k = a.shape
    n = w.shape[2]
    return pl.pallas_call(
        _mm_res_kernel,
        grid=(m // tm,),
        in_specs=[pl.BlockSpec((tm, k), lambda i: (i, 0)),
                  pl.BlockSpec((None, k, n), lambda i: (widx, 0, 0), pipeline_mode=pl.Buffered(1)),
                  pl.BlockSpec((tm, n), lambda i: (i, 0))],
        out_specs=pl.BlockSpec((tm, n), lambda i: (i, 0)),
        out_shape=jax.ShapeDtypeStruct((m, n), F32),
        compiler_params=_params(("parallel",), 40),
        name=name,
    )(a, w, res)


def _ffn_kernel(x_ref, g_ref, w1_ref, w2_ref, gf_ref, o_ref, xn_ref, *, final_norm):
    f = pl.program_id(1)

    @pl.when(f == 0)
    def _():
        _norm_rows_into(x_ref, g_ref, xn_ref)
        o_ref[...] = x_ref[...]

    h = jnp.dot(xn_ref[...], w1_ref[...], preferred_element_type=F32)
    h = jnp.maximum(h, 0.0)
    h = (h * h).astype(BF16)
    o_ref[...] += jnp.dot(h, w2_ref[...], preferred_element_type=F32)

    if final_norm:
        @pl.when(f == pl.num_programs(1) - 1)
        def _():
            rows = 256

            def body(r, c):
                sl = pl.ds(pl.multiple_of(r * rows, rows), rows)
                o_ref[sl, :] = _rms(o_ref[sl, :], gf_ref[...])
                return c
            lax.fori_loop(0, o_ref.shape[0] // rows, body, 0)


def ffn(x, g, w1, w2, widx, gf, *, tm, tf, final_norm, name):
    m, d = x.shape
    dff = w1.shape[2]
    return pl.pallas_call(
        functools.partial(_ffn_kernel, final_norm=final_norm),
        grid=(m // tm, dff // tf),
        in_specs=[pl.BlockSpec((tm, d), lambda i, f: (i, 0)),
                  pl.BlockSpec((1, d), lambda i, f: (0, 0)),
                  pl.BlockSpec((None, d, tf), lambda i, f: (widx, 0, f)),
                  pl.BlockSpec((None, tf, d), lambda i, f: (widx, f, 0)),
                  pl.BlockSpec((1, d), lambda i, f: (0, 0))],
        out_specs=pl.BlockSpec((tm, d), lambda i, f: (i, 0)),
        out_shape=jax.ShapeDtypeStruct((m, d), F32),
        scratch_shapes=[pltpu.VMEM((tm, d), BF16)],
        compiler_params=_params(("parallel", "arbitrary"), 52),
        name=name,
    )(x, g.reshape(1, d), w1, w2, gf.reshape(1, d))


def _gelu(x):
    c = 0.7978845608028654
    return 0.5 * x * (1.0 + jnp.tanh(c * (x + 0.044715 * (x * x * x))))


EVEN_CHUNK_ROWS = 256
CONV_TAIL_ROWS = 8


def _even_mixer_kernel(x_ref, g_ref, win_ref, vg_ref, ws_ref, bs_ref, cw_ref, wout_ref, o_ref, tail_ref, *, tm):
    @pl.when((pl.program_id(0) * tm) % SEQ == 0)
    def _():
        tail_ref[...] = jnp.zeros_like(tail_ref)

    ri = lax.broadcasted_iota(jnp.int32, (GMLP_BLOCK, GMLP_BLOCK), 0) // CHUNK
    ci = lax.broadcasted_iota(jnp.int32, (GMLP_BLOCK, GMLP_BLOCK), 1) // CHUNK
    keep = ci <= ri
    w_mix = [jnp.where(keep, ws_ref[g], 0.0).astype(BF16) for g in range(A_GROUPS)]
    w0 = cw_ref[0:1, :]
    w1 = cw_ref[1:2, :]
    w2 = cw_ref[2:3, :]

    tail = tail_ref[...]
    for r in range(tm // EVEN_CHUNK_ROWS):
        sl = slice(r * EVEN_CHUNK_ROWS, (r + 1) * EVEN_CHUNK_ROWS)
        x = x_ref[sl, :]
        xn = _rms(x, g_ref[...]).astype(BF16)
        z = jnp.dot(xn, win_ref[...], preferred_element_type=F32)

        cols = []
        for g in range(A_GROUPS):
            c0 = g * LANES
            u = _gelu(z[:, c0:c0 + LANES])
            v = _gelu(z[:, A_WIDTH + c0:A_WIDTH + c0 + LANES])
            vn = _rms(v, vg_ref[:, c0:c0 + LANES]).astype(BF16)
            mixed = jnp.concatenate(
                [jnp.dot(w_mix[g], vn[b0:b0 + GMLP_BLOCK], preferred_element_type=F32) + bs_ref[:, g:g + 1]
                 for b0 in range(0, EVEN_CHUNK_ROWS, GMLP_BLOCK)], axis=0)
            cols.append((u * mixed).astype(BF16))

        zc = z[:, 3 * 1024:4 * 1024] * z[:, 4 * 1024:5 * 1024]
        big = jnp.concatenate([tail, zc], axis=0)
        s1 = pltpu.roll(big, 1, axis=0)[CONV_TAIL_ROWS:]
        s2 = pltpu.roll(big, 2, axis=0)[CONV_TAIL_ROWS:]
        conv = w2 * zc + w1 * s1 + w0 * s2
        cols.append((z[:, 2 * 1024:3 * 1024] * conv).astype(BF16))
        tail = zc[EVEN_CHUNK_ROWS - CONV_TAIL_ROWS:]

        y = jnp.concatenate(cols, axis=1)
        o_ref[sl, :] = x + jnp.dot(y, wout_ref[...], preferred_element_type=F32)
    tail_ref[...] = tail


def even_mixer(x, g, w_in, v_norm_g, w_s, b_s, conv_w, w_out, widx, *, tm, name):
    t, d = x.shape
    full = lambda shape: pl.BlockSpec(shape, lambda i: (0,) * len(shape))
    resident = lambda w: pl.BlockSpec((None,) + w.shape[1:], lambda i: (widx, 0, 0), pipeline_mode=pl.Buffered(1))
    return pl.pallas_call(
        functools.partial(_even_mixer_kernel, tm=tm),
        grid=(t // tm,),
        in_specs=[pl.BlockSpec((tm, d), lambda i: (i, 0)),
                  full((1, d)), resident(w_in), full((1, A_WIDTH)),
                  full((A_GROUPS, GMLP_BLOCK, GMLP_BLOCK)), full((GMLP_BLOCK, A_GROUPS)), full((3, B_WIDTH)),
                  resident(w_out)],
        out_specs=pl.BlockSpec((tm, d), lambda i: (i, 0)),
        out_shape=jax.ShapeDtypeStruct((t, d), F32),
        scratch_shapes=[pltpu.VMEM((CONV_TAIL_ROWS, B_WIDTH), F32)],
        compiler_params=_params(("arbitrary",), 56),
        name=name,
    )(x, g.reshape(1, d), w_in, v_norm_g.reshape(1, A_WIDTH), w_s, b_s.T, conv_w, w_out)


def _rope_table_kernel(pos_ref, inv_ref, sign_ref, cs_ref, sn_ref):
    ang = pos_ref[...].astype(F32) * inv_ref[...]
    cs_ref[...] = jnp.cos(ang)
    sn_ref[...] = jnp.sin(ang) * sign_ref[...]


def rope_tables(positions, *, tm=2048):
    t = positions.size
    half = C_ROPE // 2
    inv = ROPE_THETA ** (-jnp.arange(0, C_ROPE, 2, dtype=F32) / C_ROPE)
    inv128 = jnp.tile(inv, LANES // half).reshape(1, LANES)
    sign = jnp.tile(jnp.concatenate([-jnp.ones((half,), F32), jnp.ones((half,), F32)]), 2).reshape(1, LANES)
    return pl.pallas_call(
        _rope_table_kernel,
        grid=(t // tm,),
        in_specs=[pl.BlockSpec((tm, 1), lambda i: (i, 0)),
                  pl.BlockSpec((1, LANES), lambda i: (0, 0)),
                  pl.BlockSpec((1, LANES), lambda i: (0, 0))],
        out_specs=[pl.BlockSpec((tm, LANES), lambda i: (i, 0)),
                   pl.BlockSpec((tm, LANES), lambda i: (i, 0))],
        out_shape=[jax.ShapeDtypeStruct((t, LANES), F32)] * 2,
        compiler_params=_params(("parallel",), 32),
        name="rope_tables",
    )(positions.reshape(t, 1), inv128, sign)


def _swap_halves(x):
    lane = lax.broadcasted_iota(jnp.int32, x.shape, 1)
    first = (lane % C_ROPE) < (C_ROPE // 2)
    return jnp.where(first, pltpu.roll(x, LANES - C_ROPE // 2, axis=1), pltpu.roll(x, C_ROPE // 2, axis=1))


def _mla_prep_kernel(x_ref, g_ref, win_ref, qg_ref, kvg_ref, wuq_ref, wukv_ref, cs_ref, sn_ref,
                     qn_ref, qr_ref, kn_ref, v_ref, kr_ref):
    xn = _rms(x_ref[...], g_ref[...]).astype(BF16)
    z = jnp.dot(xn, win_ref[...], preferred_element_type=F32)
    cq = _rms(z[:, :C_Q_RANK], qg_ref[...]).astype(BF16)
    ckv = _rms(z[:, C_Q_RANK:C_Q_RANK + C_KV_RANK], kvg_ref[...]).astype(BF16)
    cs = cs_ref[...]
    sn = sn_ref[...]

    kr2 = z[:, C_Q_RANK + C_KV_RANK:]
    kr2 = kr2 * cs + _swap_halves(kr2) * sn
    lane = lax.broadcasted_iota(jnp.int32, kr2.shape, 1)
    kr_ref[0] = jnp.where(lane < C_ROPE, kr2, 0.0).astype(BF16)
    kr_ref[1] = jnp.where(lane >= C_ROPE, kr2, 0.0).astype(BF16)

    q = jnp.dot(cq, wuq_ref[...], preferred_element_type=F32)
    for h in range(C_HEADS):
        qn_ref[h] = q[:, h * C_NOPE:(h + 1) * C_NOPE].astype(BF16)
    for p in range(C_HEADS // 2):
        base = C_HEADS * C_NOPE + p * LANES
        xr = q[:, base:base + LANES]
        qr_ref[p] = (xr * cs + _swap_halves(xr) * sn).astype(BF16)

    kv = jnp.dot(ckv, wukv_ref[...], preferred_element_type=F32)
    for h in range(C_HEADS):
        kn_ref[h] = kv[:, h * C_NOPE:(h + 1) * C_NOPE].astype(BF16)
        v_ref[h] = kv[:, C_HEADS * C_NOPE + h * C_V:C_HEADS * C_NOPE + (h + 1) * C_V].astype(BF16)


def mla_prep(x, g, w_in, q_g, kv_g, w_uq, w_ukv, widx, cs, sn, *, tm, name):
    t, d = x.shape
    full = lambda shape: pl.BlockSpec(shape, lambda i: (0,) * len(shape))
    layer = lambda w: pl.BlockSpec((None,) + w.shape[1:], lambda i: (widx, 0, 0))
    hspec = lambda n: pl.BlockSpec((n, tm, LANES), lambda i: (0, i, 0))
    hshape = lambda n: jax.ShapeDtypeStruct((n, t, LANES), BF16)
    return pl.pallas_call(
        _mla_prep_kernel,
        grid=(t // tm,),
        in_specs=[pl.BlockSpec((tm, d), lambda i: (i, 0)),
                  full((1, d)), layer(w_in), full((1, C_Q_RANK)), full((1, C_KV_RANK)),
                  layer(w_uq), layer(w_ukv),
                  pl.BlockSpec((tm, LANES), lambda i: (i, 0)),
                  pl.BlockSpec((tm, LANES), lambda i: (i, 0))],
        out_specs=[hspec(C_HEADS), hspec(C_HEADS // 2), hspec(C_HEADS), hspec(C_HEADS), hspec(2)],
        out_shape=[hshape(C_HEADS), hshape(C_HEADS // 2), hshape(C_HEADS), hshape(C_HEADS), hshape(2)],
        compiler_params=_params(("parallel",), 48),
        name=name,
    )(x, g.reshape(1, d), w_in, q_g.reshape(1, -1), kv_g.reshape(1, -1), w_uq, w_ukv, cs, sn)


def _mla_attn_kernel(qn_ref, qr_ref, kn_ref, v_ref, kr_ref, o_ref, *, tq):
    exp_mult = (C_NOPE + C_ROPE) ** -0.5 * 1.4426950408889634
    nq = SEQ // tq
    rq = lax.broadcasted_iota(jnp.int32, (tq, tq), 0) // CHUNK
    ck = lax.broadcasted_iota(jnp.int32, (tq, tq), 1) // CHUNK
    diag_keep = ck <= rq
    nt = (((1,), (1,)), ((), ()))

    def keys(hh, lo, hi):
        return jnp.concatenate([kn_ref[hh, lo:hi, :], kr_ref[hh, lo:hi, :]], axis=1)

    for t in reversed(range(nq)):
        for hh in range(2):
            lo, hi = t * tq, (t + 1) * tq
            q = jnp.concatenate([qn_ref[hh, lo:hi, :], qr_ref[0, lo:hi, :]], axis=1)
            s_d = lax.dot_general(q, keys(hh, lo, hi), nt, preferred_element_type=F32)
            s_d = jnp.where(diag_keep, s_d, -jnp.inf)
            m = jnp.max(s_d, axis=-1, keepdims=True)
            if t > 0:
                s_p = lax.dot_general(q, keys(hh, 0, lo), nt, preferred_element_type=F32)
                m = jnp.maximum(m, jnp.max(s_p, axis=-1, keepdims=True))
            p_d = jnp.exp2((s_d - m) * exp_mult)
            l = jnp.sum(p_d, axis=-1, keepdims=True)
            acc = jnp.dot(p_d.astype(BF16), v_ref[hh, lo:hi, :], preferred_element_type=F32)
            if t > 0:
                p_p = jnp.exp2((s_p - m) * exp_mult)
                l = l + jnp.sum(p_p, axis=-1, keepdims=True)
                acc = acc + jnp.dot(p_p.astype(BF16), v_ref[hh, 0:lo, :], preferred_element_type=F32)
            o_ref[lo:hi, hh * C_V:(hh + 1) * C_V] = (acc / l).astype(o_ref.dtype)


def mla_attention(qn, qr, kn, v, kr, *, bsz, tq, name):
    t = qn.shape[1]
    return pl.pallas_call(
        functools.partial(_mla_attn_kernel, tq=tq),
        grid=(bsz, C_HEADS // 2),
        in_specs=[pl.BlockSpec((2, SEQ, LANES), lambda b, p: (p, b, 0)),
                  pl.BlockSpec((1, SEQ, LANES), lambda b, p: (p, b, 0)),
                  pl.BlockSpec((2, SEQ, LANES), lambda b, p: (p, b, 0)),
                  pl.BlockSpec((2, SEQ, LANES), lambda b, p: (p, b, 0)),
                  pl.BlockSpec((2, SEQ, LANES), lambda b, p: (0, b, 0))],
        out_specs=pl.BlockSpec((SEQ, 2 * C_V), lambda b, p: (b, p)),
        out_shape=jax.ShapeDtypeStruct((t, C_HEADS * C_V), BF16),
        compiler_params=_params(("parallel", "parallel"), 40),
        name=name,
    )(qn, qr, kn, v, kr)


MEM_CHUNK_ROWS = 256


def _mem_block_kernel(x_ref, g_ref, wq_ref, k_ref, v_ref, wo_ref, o_ref):
    exp_mult = MEM_HEAD_DIM ** -0.5 * 1.4426950408889634
    nt = (((1,), (1,)), ((), ()))
    for r in range(x_ref.shape[0] // MEM_CHUNK_ROWS):
        sl = slice(r * MEM_CHUNK_ROWS, (r + 1) * MEM_CHUNK_ROWS)
        x = x_ref[sl, :]
        xn = _rms(x, g_ref[...]).astype(BF16)
        q = jnp.dot(xn, wq_ref[...], preferred_element_type=F32).astype(BF16)
        heads = []
        for h in range(MEM_HEADS):
            c0 = h * MEM_HEAD_DIM
            s = lax.dot_general(q[:, c0:c0 + MEM_HEAD_DIM], k_ref[:, c0:c0 + MEM_HEAD_DIM], nt,
                                preferred_element_type=F32)
            m = jnp.max(s, axis=-1, keepdims=True)
            p = jnp.exp2((s - m) * exp_mult)
            l = jnp.sum(p, axis=-1, keepdims=True)
            o = jnp.dot(p.astype(BF16), v_ref[:, c0:c0 + MEM_HEAD_DIM], preferred_element_type=F32)
            heads.append((o / l).astype(BF16))
        att = jnp.concatenate(heads, axis=1)
        o_ref[sl, :] = x + jnp.dot(att, wo_ref[...], preferred_element_type=F32)


def mem_block(x, g, wq, k, v, wo, widx, *, tm, name):
    t, d = x.shape
    per_b = SEQ // tm
    resident = lambda: pl.BlockSpec((None, d, d), lambda i: (widx, 0, 0), pipeline_mode=pl.Buffered(1))
    return pl.pallas_call(
        _mem_block_kernel,
        grid=(t // tm,),
        in_specs=[pl.BlockSpec((tm, d), lambda i: (i, 0)),
                  pl.BlockSpec((1, d), lambda i: (0, 0)),
                  resident(),
                  pl.BlockSpec((N_MEM, d), lambda i: (i // per_b, 0)),
                  pl.BlockSpec((N_MEM, d), lambda i: (i // per_b, 0)),
                  resident()],
        out_specs=pl.BlockSpec((tm, d), lambda i: (i, 0)),
        out_shape=jax.ShapeDtypeStruct((t, d), F32),
        compiler_params=_params(("parallel",), 48),
        name=name,
    )(x, g.reshape(1, d), wq, k, v, wo)


def kernel(x, mem, positions, norm_mix_g, norm_mem_q_g, norm_mem_kv_g, norm_ffn_g, final_norm_g, ab_w_in, a_v_norm_g, a_w_s, a_b_s, b_conv_w, ab_w_out, c_w_in, c_q_norm_g, c_kv_norm_g, c_w_uq, c_w_ukv, c_w_out, m_wq, m_wk, m_wv, m_wo, f_w1, f_w2):
    bsz, seq, d = x.shape
    depth = norm_mix_g.shape[0]
    t = bsz * seq
    xf = x.reshape(t, d)
    memf = mem.reshape(bsz * N_MEM, d)
    bf = lambda a: a.astype(BF16)

    w_ab_in, w_ab_out, w_c_out = bf(ab_w_in), bf(ab_w_out), bf(c_w_out)
    w_mq, w_mk, w_mv, w_mo = bf(m_wq), bf(m_wk), bf(m_wv), bf(m_wo)
    w_f1, w_f2 = bf(f_w1), bf(f_w2)
    n_odd = c_w_in.shape[0]
    w_c_in = bf(jnp.concatenate([c_w_in, c_w_in[:, :, C_Q_RANK + C_KV_RANK:]], axis=2))
    wuq = c_w_uq.reshape(n_odd, C_Q_RANK, C_HEADS, C_NOPE + C_ROPE)
    w_uq = bf(jnp.concatenate([wuq[..., :C_NOPE].reshape(n_odd, C_Q_RANK, -1),
                               wuq[..., C_NOPE:].reshape(n_odd, C_Q_RANK, -1)], axis=2))
    wukv = c_w_ukv.reshape(n_odd, C_KV_RANK, C_HEADS, C_NOPE + C_V)
    w_ukv = bf(jnp.concatenate([wukv[..., :C_NOPE].reshape(n_odd, C_KV_RANK, -1),
                                wukv[..., C_NOPE:].reshape(n_odd, C_KV_RANK, -1)], axis=2))

    cs, sn = rope_tables(positions)

    for layer in range(depth):
        if layer % 2 == 0:
            e = layer // 2
            xf = even_mixer(xf, norm_mix_g[layer], w_ab_in, a_v_norm_g[e], a_w_s[e], a_b_s[e], b_conv_w[e],
                            w_ab_out, e, tm=512, name=f"even_mixer_{layer}")
        else:
            o = layer // 2
            qn, qr, kn, v, kr = mla_prep(xf, norm_mix_g[layer], w_c_in, c_q_norm_g[o], c_kv_norm_g[o],
                                         w_uq, w_ukv, o, cs, sn, tm=512, name=f"mla_prep_{layer}")
            att = mla_attention(qn, qr, kn, v, kr, bsz=bsz, tq=512, name=f"mla_attn_{layer}")
            xf = matmul_residual(att, w_c_out, o, xf, tm=512, name=f"c_out_{layer}")

        mk = norm_matmul(memf, norm_mem_kv_g[layer], w_mk, layer, tm=512, tn=d, out_dtype=BF16, vmem_mib=32,
                         name=f"mem_k_{layer}")
        mv = norm_matmul(memf, norm_mem_kv_g[layer], w_mv, layer, tm=512, tn=d, out_dtype=BF16, vmem_mib=32,
                         name=f"mem_v_{layer}")
        xf = mem_block(xf, norm_mem_q_g[layer], w_mq, mk, mv, w_mo, layer, tm=512, name=f"mem_block_{layer}")

        xf = ffn(xf, norm_ffn_g[layer], w_f1, w_f2, layer, final_norm_g, tm=512, tf=1024,
                 final_norm=(layer == depth - 1), name=f"ffn_{layer}")

    return xf.reshape(bsz, seq, d)
```

```python
import functools

import jax
import jax.numpy as jnp
from jax import lax
from jax.experimental import pallas as pl
from jax.experimental.pallas import tpu as pltpu

F32 = jnp.float32
BF16 = jnp.bfloat16

D_MODEL = 2048
SEQ = 2048
CHUNK = 64
N_MEM = 256
RMS_EPS = 1e-6
GMLP_BLOCK = 128
A_WIDTH = 1024
A_GROUPS = 8
B_WIDTH = 1024
C_HEADS = 16
C_NOPE = 128
C_ROPE = 64
C_V = 128
C_Q_RANK = 512
C_KV_RANK = 256
ROPE_THETA = 10000.0
MEM_HEADS = 4
MEM_HEAD_DIM = 512
D_FF = 8192
LANES = 128
MIB = 1024 * 1024


def _params(semantics, vmem_mib):
    return pltpu.CompilerParams(dimension_semantics=semantics, vmem_limit_bytes=vmem_mib * MIB)


def _rms(x, g):
    ms = jnp.mean(x * x, axis=-1, keepdims=True)
    return x * lax.rsqrt(ms + RMS_EPS) * g


def _norm_rows_into(x_ref, g_ref, xn_ref, rows=256):
    def body(r, c):
        sl = pl.ds(pl.multiple_of(r * rows, rows), rows)
        xn_ref[sl, :] = _rms(x_ref[sl, :], g_ref[...]).astype(BF16)
        return c
    lax.fori_loop(0, x_ref.shape[0] // rows, body, 0)


NORM_CHUNK_ROWS = 256


def _norm_mm_kernel(x_ref, g_ref, w_ref, o_ref, *scratch, n_col_steps):
    tm = x_ref.shape[0]

    def first():
        for r in range(tm // NORM_CHUNK_ROWS):
            sl = slice(r * NORM_CHUNK_ROWS, (r + 1) * NORM_CHUNK_ROWS)
            xn = _rms(x_ref[sl, :], g_ref[...]).astype(BF16)
            if n_col_steps > 1:
                scratch[0][sl, :] = xn
            o_ref[sl, :] = jnp.dot(xn, w_ref[...], preferred_element_type=F32).astype(o_ref.dtype)

    if n_col_steps == 1:
        first()
    else:
        pl.when(pl.program_id(1) == 0)(first)

        @pl.when(pl.program_id(1) != 0)
        def _():
            o_ref[...] = jnp.dot(scratch[0][...], w_ref[...], preferred_element_type=F32).astype(o_ref.dtype)


def norm_matmul(x, g, w, widx, *, tm, tn, out_dtype, vmem_mib, name):
    m, k = x.shape
    n = w.shape[2]
    n_col_steps = n // tn
    w_mode = dict(pipeline_mode=pl.Buffered(1)) if n_col_steps == 1 else {}
    return pl.pallas_call(
        functools.partial(_norm_mm_kernel, n_col_steps=n_col_steps),
        grid=(m // tm, n_col_steps),
        in_specs=[pl.BlockSpec((tm, k), lambda i, j: (i, 0)),
                  pl.BlockSpec((1, k), lambda i, j: (0, 0)),
                  pl.BlockSpec((None, k, tn), lambda i, j: (widx, 0, j), **w_mode)],
        out_specs=pl.BlockSpec((tm, tn), lambda i, j: (i, j)),
        out_shape=jax.ShapeDtypeStruct((m, n), out_dtype),
        scratch_shapes=[pltpu.VMEM((tm, k), BF16)] if n_col_steps > 1 else [],
        compiler_params=_params(("parallel", "arbitrary"), vmem_mib),
        name=name,
    )(x, g.reshape(1, k), w)


def _norm_mm_pair_kernel(x_ref, g_ref, wa_ref, wb_ref, oa_ref, ob_ref):
    for r in range(x_ref.shape[0] // NORM_CHUNK_ROWS):
        sl = slice(r * NORM_CHUNK_ROWS, (r + 1) * NORM_CHUNK_ROWS)
        xn = _rms(x_ref[sl, :], g_ref[...]).astype(BF16)
        oa_ref[sl, :] = jnp.dot(xn, wa_ref[...], preferred_element_type=F32).astype(oa_ref.dtype)
        ob_ref[sl, :] = jnp.dot(xn, wb_ref[...], preferred_element_type=F32).astype(ob_ref.dtype)


def norm_matmul_pair(x, g, wa, wb, widx, *, tm, name):
    m, k = x.shape
    n = wa.shape[2]
    resident = lambda: pl.BlockSpec((None, k, n), lambda i: (widx, 0, 0), pipeline_mode=pl.Buffered(1))
    return pl.pallas_call(
        _norm_mm_pair_kernel,
        grid=(m // tm,),
        in_specs=[pl.BlockSpec((tm, k), lambda i: (i, 0)),
                  pl.BlockSpec((1, k), lambda i: (0, 0)),
                  resident(), resident()],
        out_specs=[pl.BlockSpec((tm, n), lambda i: (i, 0))] * 2,
        out_shape=[jax.ShapeDtypeStruct((m, n), BF16)] * 2,
        compiler_params=_params(("parallel",), 40),
        name=name,
    )(x, g.reshape(1, k), wa, wb)


def _mm_res_kernel(a_ref, w_ref, r_ref, o_ref):
    o_ref[...] = r_ref[...] + jnp.dot(a_ref[...], w_ref[...], preferred_element_type=F32)


def matmul_residual(a, w, widx, res, *, tm, name):
    m, k = a.shape
    n = w.shape[2]
    return pl.pallas_call(
        _mm_res_kernel,
        grid=(m // tm,),
        in_specs=[pl.BlockSpec((tm, k), lambda i: (i, 0)),
                  pl.BlockSpec((None, k, n), lambda i: (widx, 0, 0), pipeline_mode=pl.Buffered(1)),
                  pl.BlockSpec((tm, n), lambda i: (i, 0))],
        out_specs=pl.BlockSpec((tm, n), lambda i: (i, 0)),
        out_shape=jax.ShapeDtypeStruct((m, n), F32),
        compiler_params=_params(("parallel",), 40),
        name=name,
    )(a, w, res)


def _ffn_kernel(x_ref, g_ref, w1_ref, w2_ref, gf_ref, o_ref, xn_ref, *, final_norm):
    f = pl.program_id(1)

    @pl.when(f == 0)
    def _():
        _norm_rows_into(x_ref, g_ref, xn_ref)
        o_ref[...] = x_ref[...]

    h = jnp.dot(xn_ref[...], w1_ref[...], preferred_element_type=F32)
    h = jnp.maximum(h, 0.0)
    h = (h * h).astype(BF16)
    o_ref[...] += jnp.dot(h, w2_ref[...], preferred_element_type=F32)

    if final_norm:
        @pl.when(f == pl.num_programs(1) - 1)
        def _():
            rows = 256

            def body(r, c):
                sl = pl.ds(pl.multiple_of(r * rows, rows), rows)
                o_ref[sl, :] = _rms(o_ref[sl, :], gf_ref[...])
                return c
            lax.fori_loop(0, o_ref.shape[0] // rows, body, 0)


def ffn(x, g, w1, w2, widx, gf, *, tm, tf, final_norm, name):
    m, d = x.shape
    dff = w1.shape[2]
    return pl.pallas_call(
        functools.partial(_ffn_kernel, final_norm=final_norm),
        grid=(m // tm, dff // tf),
        in_specs=[pl.BlockSpec((tm, d), lambda i, f: (i, 0)),
                  pl.BlockSpec((1, d), lambda i, f: (0, 0)),
                  pl.BlockSpec((None, d, tf), lambda i, f: (widx, 0, f)),
                  pl.BlockSpec((None, tf, d), lambda i, f: (widx, f, 0)),
                  pl.BlockSpec((1, d), lambda i, f: (0, 0))],
        out_specs=pl.BlockSpec((tm, d), lambda i, f: (i, 0)),
        out_shape=jax.ShapeDtypeStruct((m, d), F32),
        scratch_shapes=[pltpu.VMEM((tm, d), BF16)],
        compiler_params=_params(("parallel", "arbitrary"), 52),
        name=name,
    )(x, g.reshape(1, d), w1, w2, gf.reshape(1, d))


def _gelu(x):
    c = 0.7978845608028654
    return 0.5 * x * (1.0 + jnp.tanh(c * (x + 0.044715 * (x * x * x))))


EVEN_CHUNK_ROWS = 256
CONV_TAIL_ROWS = 8


def _even_mixer_kernel(x_ref, g_ref, win_ref, vg_ref, ws_ref, bs_ref, cw_ref, wout_ref, o_ref, tail_ref, *, tm):
    @pl.when((pl.program_id(0) * tm) % SEQ == 0)
    def _():
        tail_ref[...] = jnp.zeros_like(tail_ref)

    ri = lax.broadcasted_iota(jnp.int32, (GMLP_BLOCK, GMLP_BLOCK), 0) // CHUNK
    ci = lax.broadcasted_iota(jnp.int32, (GMLP_BLOCK, GMLP_BLOCK), 1) // CHUNK
    keep = ci <= ri
    w_mix = [jnp.where(keep, ws_ref[g], 0.0).astype(BF16) for g in range(A_GROUPS)]
    w0 = cw_ref[0:1, :]
    w1 = cw_ref[1:2, :]
    w2 = cw_ref[2:3, :]

    tail = tail_ref[...]
    for r in range(tm // EVEN_CHUNK_ROWS):
        sl = slice(r * EVEN_CHUNK_ROWS, (r + 1) * EVEN_CHUNK_ROWS)
        x = x_ref[sl, :]
        xn = _rms(x, g_ref[...]).astype(BF16)
        z = jnp.dot(xn, win_ref[...], preferred_element_type=F32)

        cols = []
        for g in range(A_GROUPS):
            c0 = g * LANES
            u = _gelu(z[:, c0:c0 + LANES])
            v = _gelu(z[:, A_WIDTH + c0:A_WIDTH + c0 + LANES])
            vn = _rms(v, vg_ref[:, c0:c0 + LANES]).astype(BF16)
            mixed = jnp.concatenate(
                [jnp.dot(w_mix[g], vn[b0:b0 + GMLP_BLOCK], preferred_element_type=F32) + bs_ref[:, g:g + 1]
                 for b0 in range(0, EVEN_CHUNK_ROWS, GMLP_BLOCK)], axis=0)
            cols.append((u * mixed).astype(BF16))

        zc = z[:, 3 * 1024:4 * 1024] * z[:, 4 * 1024:5 * 1024]
        big = jnp.concatenate([tail, zc], axis=0)
        s1 = pltpu.roll(big, 1, axis=0)[CONV_TAIL_ROWS:]
        s2 = pltpu.roll(big, 2, axis=0)[CONV_TAIL_ROWS:]
        conv = w2 * zc + w1 * s1 + w0 * s2
        cols.append((z[:, 2 * 1024:3 * 1024] * conv).astype(BF16))
        tail = zc[EVEN_CHUNK_ROWS - CONV_TAIL_ROWS:]

        y = jnp.concatenate(cols, axis=1)
        o_ref[sl, :] = x + jnp.dot(y, wout_ref[...], preferred_element_type=F32)
    tail_ref[...] = tail


def even_mixer(x, g, w_in, v_norm_g, w_s, b_s, conv_w, w_out, widx, *, tm, name):
    t, d = x.shape
    full = lambda shape: pl.BlockSpec(shape, lambda i: (0,) * len(shape))
    resident = lambda w: pl.BlockSpec((None,) + w.shape[1:], lambda i: (widx, 0, 0), pipeline_mode=pl.Buffered(1))
    return pl.pallas_call(
        functools.partial(_even_mixer_kernel, tm=tm),
        grid=(t // tm,),
        in_specs=[pl.BlockSpec((tm, d), lambda i: (i, 0)),
                  full((1, d)), resident(w_in), full((1, A_WIDTH)),
                  full((A_GROUPS, GMLP_BLOCK, GMLP_BLOCK)), full((GMLP_BLOCK, A_GROUPS)), full((3, B_WIDTH)),
                  resident(w_out)],
        out_specs=pl.BlockSpec((tm, d), lambda i: (i, 0)),
        out_shape=jax.ShapeDtypeStruct((t, d), F32),
        scratch_shapes=[pltpu.VMEM((CONV_TAIL_ROWS, B_WIDTH), F32)],
        compiler_params=_params(("arbitrary",), 56),
        name=name,
    )(x, g.reshape(1, d), w_in, v_norm_g.reshape(1, A_WIDTH), w_s, b_s.T, conv_w, w_out)


def _rope_table_kernel(pos_ref, inv_ref, sign_ref, cs_ref, sn_ref):
    ang = pos_ref[...].astype(F32) * inv_ref[...]
    cs_ref[...] = jnp.cos(ang)
    sn_ref[...] = jnp.sin(ang) * sign_ref[...]


def rope_tables(positions, *, tm=2048):
    t = positions.size
    half = C_ROPE // 2
    inv = ROPE_THETA ** (-jnp.arange(0, C_ROPE, 2, dtype=F32) / C_ROPE)
    inv128 = jnp.tile(inv, LANES // half).reshape(1, LANES)
    sign = jnp.tile(jnp.concatenate([-jnp.ones((half,), F32), jnp.ones((half,), F32)]), 2).reshape(1, LANES)
    return pl.pallas_call(
        _rope_table_kernel,
        grid=(t // tm,),
        in_specs=[pl.BlockSpec((tm, 1), lambda i: (i, 0)),
                  pl.BlockSpec((1, LANES), lambda i: (0, 0)),
                  pl.BlockSpec((1, LANES), lambda i: (0, 0))],
        out_specs=[pl.BlockSpec((tm, LANES), lambda i: (i, 0)),
                   pl.BlockSpec((tm, LANES), lambda i: (i, 0))],
        out_shape=[jax.ShapeDtypeStruct((t, LANES), F32)] * 2,
        compiler_params=_params(("parallel",), 32),
        name="rope_tables",
    )(positions.reshape(t, 1), inv128, sign)


def _swap_halves(x):
    lane = lax.broadcasted_iota(jnp.int32, x.shape, 1)
    first = (lane % C_ROPE) < (C_ROPE // 2)
    return jnp.where(first, pltpu.roll(x, LANES - C_ROPE // 2, axis=1), pltpu.roll(x, C_ROPE // 2, axis=1))


def _mla_prep_kernel(x_ref, g_ref, win_ref, qg_ref, kvg_ref, wuq_ref, wukv_ref, cs_ref, sn_ref,
                     qn_ref, qr_ref, kn_ref, v_ref, kr_ref):
    xn = _rms(x_ref[...], g_ref[...]).astype(BF16)
    z = jnp.dot(xn, win_ref[...], preferred_element_type=F32)
    cq = _rms(z[:, :C_Q_RANK], qg_ref[...]).astype(BF16)
    ckv = _rms(z[:, C_Q_RANK:C_Q_RANK + C_KV_RANK], kvg_ref[...]).astype(BF16)
    cs = cs_ref[...]
    sn = sn_ref[...]

    kr2 = z[:, C_Q_RANK + C_KV_RANK:]
    kr2 = kr2 * cs + _swap_halves(kr2) * sn
    lane = lax.broadcasted_iota(jnp.int32, kr2.shape, 1)
    kr_ref[0] = jnp.where(lane < C_ROPE, kr2, 0.0).astype(BF16)
    kr_ref[1] = jnp.where(lane >= C_ROPE, kr2, 0.0).astype(BF16)

    q = jnp.dot(cq, wuq_ref[...], preferred_element_type=F32)
    for h in range(C_HEADS):
        qn_ref[h] = q[:, h * C_NOPE:(h + 1) * C_NOPE].astype(BF16)
    for p in range(C_HEADS // 2):
        base = C_HEADS * C_NOPE + p * LANES
        xr = q[:, base:base + LANES]
        qr_ref[p] = (xr * cs + _swap_halves(xr) * sn).astype(BF16)

    kv = jnp.dot(ckv, wukv_ref[...], preferred_element_type=F32)
    for h in range(C_HEADS):
        kn_ref[h] = kv[:, h * C_NOPE:(h + 1) * C_NOPE].astype(BF16)
        v_ref[h] = kv[:, C_HEADS * C_NOPE + h * C_V:C_HEADS * C_NOPE + (h + 1) * C_V].astype(BF16)


def mla_prep(x, g, w_in, q_g, kv_g, w_uq, w_ukv, widx, cs, sn, *, tm, name):
    t, d = x.shape
    full = lambda shape: pl.BlockSpec(shape, lambda i: (0,) * len(shape))
    layer = lambda w: pl.BlockSpec((None,) + w.shape[1:], lambda i: (widx, 0, 0))
    hspec = lambda n: pl.BlockSpec((n, tm, LANES), lambda i: (0, i, 0))
    hshape = lambda n: jax.ShapeDtypeStruct((n, t, LANES), BF16)
    return pl.pallas_call(
        _mla_prep_kernel,
        grid=(t // tm,),
        in_specs=[pl.BlockSpec((tm, d), lambda i: (i, 0)),
                  full((1, d)), layer(w_in), full((1, C_Q_RANK)), full((1, C_KV_RANK)),
                  layer(w_uq), layer(w_ukv),
                  pl.BlockSpec((tm, LANES), lambda i: (i, 0)),
                  pl.BlockSpec((tm, LANES), lambda i: (i, 0))],
        out_specs=[hspec(C_HEADS), hspec(C_HEADS // 2), hspec(C_HEADS), hspec(C_HEADS), hspec(2)],
        out_shape=[hshape(C_HEADS), hshape(C_HEADS // 2), hshape(C_HEADS), hshape(C_HEADS), hshape(2)],
        compiler_params=_params(("parallel",), 48),
        name=name,
    )(x, g.reshape(1, d), w_in, q_g.reshape(1, -1), kv_g.reshape(1, -1), w_uq, w_ukv, cs, sn)


def _mla_attn_kernel(qn_ref, qr_ref, kn_ref, v_ref, kr_ref, o_ref, *, tq):
    exp_mult = (C_NOPE + C_ROPE) ** -0.5 * 1.4426950408889634
    nq = SEQ // tq
    rq = lax.broadcasted_iota(jnp.int32, (tq, tq), 0) // CHUNK
    ck = lax.broadcasted_iota(jnp.int32, (tq, tq), 1) // CHUNK
    diag_keep = ck <= rq
    nt = (((1,), (1,)), ((), ()))

    def keys(hh, lo, hi):
        return jnp.concatenate([kn_ref[hh, lo:hi, :], kr_ref[hh, lo:hi, :]], axis=1)

    for t in reversed(range(nq)):
        for hh in range(2):
            lo, hi = t * tq, (t + 1) * tq
            q = jnp.concatenate([qn_ref[hh, lo:hi, :], qr_ref[0, lo:hi, :]], axis=1)
            s_d = lax.dot_general(q, keys(hh, lo, hi), nt, preferred_element_type=F32)
            s_d = jnp.where(diag_keep, s_d, -jnp.inf)
            m = jnp.max(s_d, axis=-1, keepdims=True)
            if t > 0:
                s_p = lax.dot_general(q, keys(hh, 0, lo), nt, preferred_element_type=F32)
                m = jnp.maximum(m, jnp.max(s_p, axis=-1, keepdims=True))
            p_d = jnp.exp2((s_d - m) * exp_mult)
            l = jnp.sum(p_d, axis=-1, keepdims=True)
            acc = jnp.dot(p_d.astype(BF16), v_ref[hh, lo:hi, :], preferred_element_type=F32)
            if t > 0:
                p_p = jnp.exp2((s_p - m) * exp_mult)
                l = l + jnp.sum(p_p, axis=-1, keepdims=True)
                acc = acc + jnp.dot(p_p.astype(BF16), v_ref[hh, 0:lo, :], preferred_element_type=F32)
            o_ref[lo:hi, hh * C_V:(hh + 1) * C_V] = (acc / l).astype(o_ref.dtype)


def mla_attention(qn, qr, kn, v, kr, *, bsz, tq, name):
    t = qn.shape[1]
    return pl.pallas_call(
        functools.partial(_mla_attn_kernel, tq=tq),
        grid=(bsz, C_HEADS // 2),
        in_specs=[pl.BlockSpec((2, SEQ, LANES), lambda b, p: (p, b, 0)),
                  pl.BlockSpec((1, SEQ, LANES), lambda b, p: (p, b, 0)),
                  pl.BlockSpec((2, SEQ, LANES), lambda b, p: (p, b, 0)),
                  pl.BlockSpec((2, SEQ, LANES), lambda b, p: (p, b, 0)),
                  pl.BlockSpec((2, SEQ, LANES), lambda b, p: (0, b, 0))],
        out_specs=pl.BlockSpec((SEQ, 2 * C_V), lambda b, p: (b, p)),
        out_shape=jax.ShapeDtypeStruct((t, C_HEADS * C_V), BF16),
        compiler_params=_params(("parallel", "parallel"), 40),
        name=name,
    )(qn, qr, kn, v, kr)


MEM_CHUNK_ROWS = 256


def _mem_block_kernel(x_ref, g_ref, wq_ref, k_ref, v_ref, wo_ref, o_ref):
    exp_mult = MEM_HEAD_DIM ** -0.5 * 1.4426950408889634
    nt = (((1,), (1,)), ((), ()))
    for r in range(x_ref.shape[0] // MEM_CHUNK_ROWS):
        sl = slice(r * MEM_CHUNK_ROWS, (r + 1) * MEM_CHUNK_ROWS)
        x = x_ref[sl, :]
        xn = _rms(x, g_ref[...]).astype(BF16)
        q = jnp.dot(xn, wq_ref[...], preferred_element_type=F32).astype(BF16)
        heads = []
        for h in range(MEM_HEADS):
            c0 = h * MEM_HEAD_DIM
            s = lax.dot_general(q[:, c0:c0 + MEM_HEAD_DIM], k_ref[:, c0:c0 + MEM_HEAD_DIM], nt,
                                preferred_element_type=F32)
            m = jnp.max(s, axis=-1, keepdims=True)
            p = jnp.exp2((s - m) * exp_mult)
            l = jnp.sum(p, axis=-1, keepdims=True)
            o = jnp.dot(p.astype(BF16), v_ref[:, c0:c0 + MEM_HEAD_DIM], preferred_element_type=F32)
            heads.append((o / l).astype(BF16))
        att = jnp.concatenate(heads, axis=1)
        o_ref[sl, :] = x + jnp.dot(att, wo_ref[...], preferred_element_type=F32)


def mem_block(x, g, wq, k, v, wo, widx, *, tm, name):
    t, d = x.shape
    per_b = SEQ // tm
    resident = lambda: pl.BlockSpec((None, d, d), lambda i: (widx, 0, 0), pipeline_mode=pl.Buffered(1))
    return pl.pallas_call(
        _mem_block_kernel,
        grid=(t // tm,),
        in_specs=[pl.BlockSpec((tm, d), lambda i: (i, 0)),
                  pl.BlockSpec((1, d), lambda i: (0, 0)),
                  resident(),
                  pl.BlockSpec((N_MEM, d), lambda i: (i // per_b, 0)),
                  pl.BlockSpec((N_MEM, d), lambda i: (i // per_b, 0)),
                  resident()],
        out_specs=pl.BlockSpec((tm, d), lambda i: (i, 0)),
        out_shape=jax.ShapeDtypeStruct((t, d), F32),
        compiler_params=_params(("parallel",), 48),
        name=name,
    )(x, g.reshape(1, d), wq, k, v, wo)


def kernel(x, mem, positions, norm_mix_g, norm_mem_q_g, norm_mem_kv_g, norm_ffn_g, final_norm_g, ab_w_in, a_v_norm_g, a_w_s, a_b_s, b_conv_w, ab_w_out, c_w_in, c_q_norm_g, c_kv_norm_g, c_w_uq, c_w_ukv, c_w_out, m_wq, m_wk, m_wv, m_wo, f_w1, f_w2):
    bsz, seq, d = x.shape
    depth = norm_mix_g.shape[0]
    t = bsz * seq
    xf = x.reshape(t, d)
    memf = mem.reshape(bsz * N_MEM, d)
    bf = lambda a: a.astype(BF16)

    w_ab_in, w_ab_out, w_c_out = bf(ab_w_in), bf(ab_w_out), bf(c_w_out)
    w_mq, w_mk, w_mv, w_mo = bf(m_wq), bf(m_wk), bf(m_wv), bf(m_wo)
    w_f1, w_f2 = bf(f_w1), bf(f_w2)
    n_odd = c_w_in.shape[0]
    w_c_in = bf(jnp.concatenate([c_w_in, c_w_in[:, :, C_Q_RANK + C_KV_RANK:]], axis=2))
    wuq = c_w_uq.reshape(n_odd, C_Q_RANK, C_HEADS, C_NOPE + C_ROPE)
    w_uq = bf(jnp.concatenate([wuq[..., :C_NOPE].reshape(n_odd, C_Q_RANK, -1),
                               wuq[..., C_NOPE:].reshape(n_odd, C_Q_RANK, -1)], axis=2))
    wukv = c_w_ukv.reshape(n_odd, C_KV_RANK, C_HEADS, C_NOPE + C_V)
    w_ukv = bf(jnp.concatenate([wukv[..., :C_NOPE].reshape(n_odd, C_KV_RANK, -1),
                                wukv[..., C_NOPE:].reshape(n_odd, C_KV_RANK, -1)], axis=2))

    cs, sn = rope_tables(positions)

    for layer in range(depth):
        if layer % 2 == 0:
            e = layer // 2
            xf = even_mixer(xf, norm_mix_g[layer], w_ab_in, a_v_norm_g[e], a_w_s[e], a_b_s[e], b_conv_w[e],
                            w_ab_out, e, tm=512, name=f"even_mixer_{layer}")
        else:
            o = layer // 2
            qn, qr, kn, v, kr = mla_prep(xf, norm_mix_g[layer], w_c_in, c_q_norm_g[o], c_kv_norm_g[o],
                                         w_uq, w_ukv, o, cs, sn, tm=512, name=f"mla_prep_{layer}")
            att = mla_attention(qn, qr, kn, v, kr, bsz=bsz, tq=512, name=f"mla_attn_{layer}")
            xf = matmul_residual(att, w_c_out, o, xf, tm=512, name=f"c_out_{layer}")

        mk, mv = norm_matmul_pair(memf, norm_mem_kv_g[layer], w_mk, w_mv, layer, tm=512, name=f"mem_kv_{layer}")
        xf = mem_block(xf, norm_mem_q_g[layer], w_mq, mk, mv, w_mo, layer, tm=512, name=f"mem_block_{layer}")

        xf = ffn(xf, norm_ffn_g[layer], w_f1, w_f2, layer, final_norm_g, tm=512, tf=1024,
                 final_norm=(layer == depth - 1), name=f"ffn_{layer}")

    return xf.reshape(bsz, seq, d)
```
